```python
import jax, jax.numpy as jnp
from jax import lax
import numpy as np

D_MODEL = 1024
BATCH = 16
SEQ = 256
DEPTH = 2
DEC_BATCH = 4
DEC_SEQ = 1024
PAST_LEN = 256

GRID_W = 64
HEAD_DIM = 64
NA_HEADS = 4
NA_WIN_ROWS = 8
NA_WIN_COLS = 16
GQA_Q_HEADS = 8
GQA_KV_HEADS = 2
ML_HEADS = 4
ML_DK = 64
ML_DV = 64
ML_CHUNK = 64
ML_FORGET_BIAS = 3.0
N_BRANCH = 3
N_EXPERTS = 32
TOP_K = 4
D_FF_EXPERT = 1024
SWIGLU_LIMIT = 7.0
SWIGLU_ALPHA = 1.702
Q_BLOCK = 128
ROPE_THETA = 10000.0
RMS_EPS = 1e-6
NEG_INF = -1e30

NA_W = NA_HEADS * HEAD_DIM
GQ_W = GQA_Q_HEADS * HEAD_DIM
GKV_W = GQA_KV_HEADS * HEAD_DIM
MLK_W = ML_HEADS * ML_DK
MLV_W = ML_HEADS * ML_DV
ML_GATES = 4 * ML_HEADS
IN_SIZES = (NA_W, NA_W, NA_W, GQ_W, GKV_W, GKV_W, MLK_W, MLK_W, MLV_W, MLV_W, ML_GATES, N_BRANCH * D_MODEL)
IN_COLS = 3 * NA_W + GQ_W + 2 * GKV_W + 2 * MLK_W + 2 * MLV_W + ML_GATES + N_BRANCH * D_MODEL

kernel_name = 'hybrid_diffusion_na_gqa_mlstm_moe_step'

F32 = jnp.float32


def rms_norm(x, g):
    xf = x.astype(F32)
    y = xf * lax.rsqrt(jnp.mean(xf * xf, axis=-1, keepdims=True) + RMS_EPS)
    return (y * g.astype(F32)).astype(x.dtype)


def adaln(cond, w, b):
    mod = jax.nn.silu(cond) @ w + b
    return jnp.split(mod[:, None, :], 6, axis=-1)


def modulate(x, shift, scale):
    return x * (1 + scale) + shift


def rope_2d(x):
    b, s, h, d = x.shape
    nf = d // 4
    t = jnp.arange(s)
    pos = jnp.stack([t // GRID_W, t % GRID_W], axis=-1).astype(F32)
    inv_freq = ROPE_THETA ** (-jnp.arange(nf, dtype=F32) / nf)
    ang = pos[:, :, None] * inv_freq
    cos = jnp.cos(ang)[None, :, None]
    sin = jnp.sin(ang)[None, :, None]
    xr = x.astype(F32).reshape(b, s, h, 2, 2, nf)
    x1 = xr[..., 0, :]
    x2 = xr[..., 1, :]
    out = jnp.stack([x1 * cos - x2 * sin, x1 * sin + x2 * cos], axis=-2)
    return out.reshape(b, s, h, d).astype(x.dtype)


def block_attention(q, k, v):
    b, sq, hq, d = q.shape
    hkv = k.shape[2]
    g = hq // hkv
    nb = sq // Q_BLOCK
    scale = d ** -0.5
    qb = q.reshape(b, nb, Q_BLOCK, hkv, g, d).swapaxes(0, 1)

    def attend_block(qblk):
        s = jnp.einsum('bqhgd,bkhd->bhgqk', qblk, k).astype(F32) * scale
        p = jax.nn.softmax(s, axis=-1).astype(v.dtype)
        return jnp.einsum('bhgqk,bkhd->bqhgd', p, v)

    o = lax.map(attend_block, qb)
    return o.swapaxes(0, 1).reshape(b, sq, hq, d)


def neighborhood_attention(q, k, v, ctx_k, ctx_v, rel_bias):
    b, s, h, d = q.shape
    rows = s // GRID_W
    kr = min(NA_WIN_ROWS, rows)
    kc = min(NA_WIN_COLS, GRID_W)
    scale = d ** -0.5
    r = jnp.arange(rows)
    row_start = jnp.clip(r - kr // 2, 0, rows - kr)
    key_rows = row_start[:, None] + jnp.arange(kr)[None, :]
    n_loc = kr * GRID_W
    kg = jnp.take(k.reshape(b, rows, GRID_W, h, d), key_rows, axis=1).reshape(b, rows, n_loc, h, d)
    vg = jnp.take(v.reshape(b, rows, GRID_W, h, d), key_rows, axis=1).reshape(b, rows, n_loc, h, d)
    qg = q.reshape(b, rows, GRID_W, h, d)
    j = jnp.arange(GRID_W)
    col_start = jnp.clip(j - kc // 2, 0, GRID_W - kc)
    col_valid = (j[None, :] >= col_start[:, None]) & (j[None, :] < col_start[:, None] + kc)
    dr_idx = key_rows - r[:, None] + NA_WIN_ROWS - 1
    dc_idx = jnp.clip(j[None, :] - j[:, None] + NA_WIN_COLS - 1, 0, 2 * NA_WIN_COLS - 2)
    bias = rel_bias.astype(F32)[:, dr_idx[:, None, :, None], dc_idx[None, :, None, :]].reshape(h, rows, GRID_W, n_loc)
    mask = jnp.broadcast_to(col_valid[:, None, :], (GRID_W, kr, GRID_W)).reshape(GRID_W, n_loc)
    s_loc = jnp.einsum('brqhd,brkhd->bhrqk', qg, kg).astype(F32) * scale + bias[None]
    s_loc = jnp.where(mask, s_loc, NEG_INF)
    s_ctx = jnp.einsum('brqhd,bkhd->bhrqk', qg, ctx_k).astype(F32) * scale
    p = jax.nn.softmax(jnp.concatenate([s_loc, s_ctx], axis=-1), axis=-1).astype(v.dtype)
    o = (jnp.einsum('bhrqk,brkhd->brqhd', p[..., :n_loc], vg)
         + jnp.einsum('bhrqk,bkhd->brqhd', p[..., n_loc:], ctx_v))
    return o.reshape(b, s, h, d)


def mlstm_chunkwise(q, k, v, i_pre, f_pre, c0, n0, m0):
    b, s, nh, dk = q.shape
    dv = v.shape[-1]
    L = ML_CHUNK
    nc = s // L
    chunk = lambda t: t.reshape((b, nc, L, nh) + t.shape[3:]).swapaxes(2, 3).swapaxes(0, 1)
    lf = jax.nn.log_sigmoid(f_pre)
    tril = jnp.tril(jnp.ones((L, L), dtype=bool))

    def chunk_step(carry, inp):
        c, n, m = carry
        qc, kc, vc, lfc, igc = inp
        bcum = jnp.cumsum(lfc, axis=-1)
        log_d = jnp.where(tril, bcum[..., :, None] - bcum[..., None, :] + igc[..., None, :], NEG_INF)
        m_inter = bcum + m[..., None]
        m_s = jnp.maximum(m_inter, jnp.max(log_d, axis=-1))
        w_intra = jnp.exp(log_d - m_s[..., None])
        w_inter = jnp.exp(m_inter - m_s)
        qk = jnp.einsum('bhsd,bhud->bhsu', qc, kc) * w_intra
        num = jnp.einsum('bhsu,bhuv->bhsv', qk, vc) + w_inter[..., None] * jnp.einsum('bhsd,bhdv->bhsv', qc, c)
        den = jnp.sum(qk, axis=-1) + w_inter * jnp.einsum('bhsd,bhd->bhs', qc, n)
        h_out = num / jnp.maximum(jnp.abs(den), jnp.exp(-m_s))[..., None]
        b_last = bcum[..., -1]
        log_w = b_last[..., None] - bcum + igc
        m_new = jnp.maximum(b_last + m, jnp.max(log_w, axis=-1))
        w_state = jnp.exp(log_w - m_new[..., None])
        decay = jnp.exp(b_last + m - m_new)
        c_new = decay[..., None, None] * c + jnp.einsum('bhu,bhud,bhuv->bhdv', w_state, kc, vc)
        n_new = decay[..., None] * n + jnp.einsum('bhu,bhud->bhd', w_state, kc)
        return (c_new, n_new, m_new), h_out

    carry, hs = lax.scan(chunk_step, (c0.astype(F32), n0.astype(F32), m0.astype(F32)),
                         (chunk(q), chunk(k), chunk(v), chunk(lf), chunk(i_pre)))
    h_seq = hs.swapaxes(0, 1).swapaxes(2, 3).reshape(b, s, nh, dv)
    return h_seq, carry


def mlstm_branch(mq, mk, mv, mo, mg, lp, state_fwd, state_bwd):
    b, s = mq.shape[:2]
    q = mq.astype(F32)
    k = mk.astype(F32) * ML_DK ** -0.5
    v = mv.astype(F32)
    g = mg.astype(F32) + lp['gate_bias'].astype(F32)
    h_f, st_f = mlstm_chunkwise(q, k, v, g[:, :, 0], g[:, :, 1], *state_fwd)
    rev = lambda t: jnp.flip(t, axis=1)
    h_b, st_b = mlstm_chunkwise(rev(q), rev(k), rev(v), rev(g[:, :, 2]), rev(g[:, :, 3]), *state_bwd)
    h = rms_norm(h_f + rev(h_b), lp['ml_norm'].reshape(ML_HEADS, ML_DV))
    out = h.reshape(b, s, MLV_W) * jax.nn.sigmoid(mo.astype(F32))
    return out.astype(mo.dtype), st_f, st_b


def project(h, w_in_l):
    b, s, _ = h.shape
    offs = np.cumsum(np.array(IN_SIZES))[:-1].tolist()
    na_q, na_k, na_v, gq, gk, gv, mq, mk, mv, mo, mg, bg = jnp.split(h @ w_in_l, offs, axis=-1)
    heads = lambda t, n: t.reshape(b, s, n, -1)
    return (heads(na_q, NA_HEADS), heads(na_k, NA_HEADS), heads(na_v, NA_HEADS),
            heads(gq, GQA_Q_HEADS), heads(gk, GQA_KV_HEADS), heads(gv, GQA_KV_HEADS),
            heads(mq, ML_HEADS), heads(mk, ML_HEADS), heads(mv, ML_HEADS), mo,
            mg.reshape(b, s, 4, ML_HEADS), bg.reshape(b, s, N_BRANCH, D_MODEL))


def merge_branches(bg, o_na, o_gqa, o_ml, lp):
    gate = jax.nn.sigmoid(bg.astype(F32)).astype(bg.dtype)
    z = (gate[:, :, 0] * (o_na @ lp['w_br_na']) + gate[:, :, 1] * (o_gqa @ lp['w_br_gqa'])
         + gate[:, :, 2] * (o_ml @ lp['w_br_ml']))
    return z @ lp['w_out']


def mixer_context(h, lp):
    b, s, _ = h.shape
    na_q, na_k, na_v, gq, gk, gv, mq, mk, mv, mo, mg, bg = project(h, lp['w_in'])
    o_na = block_attention(na_q, na_k, na_v).reshape(b, s, NA_W)
    qn = rms_norm(gq, lp['qk_norm'][0])
    kn = rms_norm(gk, lp['qk_norm'][1])
    o_gqa = block_attention(qn, kn, gv).reshape(b, s, GQ_W)
    zero = (jnp.zeros((b, ML_HEADS, ML_DK, ML_DV), F32), jnp.zeros((b, ML_HEADS, ML_DK), F32),
            jnp.zeros((b, ML_HEADS), F32))
    o_ml, st_f, st_b = mlstm_branch(mq, mk, mv, mo, mg, lp, zero, zero)
    out = merge_branches(bg, o_na, o_gqa, o_ml, lp)
    ml_c = jnp.stack([st_f[0], st_b[0]], axis=1)
    ml_n = jnp.stack([st_f[1], st_b[1]], axis=1)
    ml_m = jnp.stack([st_f[2], st_b[2]], axis=1)
    return out, (na_k, na_v, kn, gv, ml_c, ml_n, ml_m)


def mixer_latent(h, lp, cache):
    ck_na, cv_na, ck_g, cv_g, st_c, st_n, st_m = cache
    b, s, _ = h.shape
    na_q, na_k, na_v, gq, gk, gv, mq, mk, mv, mo, mg, bg = project(h, lp['w_in'])
    o_na = neighborhood_attention(na_q, na_k, na_v, ck_na, cv_na, lp['rel_bias']).reshape(b, s, NA_W)
    qn = rope_2d(rms_norm(gq, lp['qk_norm'][0]))
    kn = rope_2d(rms_norm(gk, lp['qk_norm'][1]))
    k_all = jnp.concatenate([kn, ck_g.astype(kn.dtype)], axis=1)
    v_all = jnp.concatenate([gv, cv_g.astype(gv.dtype)], axis=1)
    o_gqa = block_attention(qn, k_all, v_all).reshape(b, s, GQ_W)
    state_fwd = (st_c[:, 0], st_n[:, 0], st_m[:, 0])
    state_bwd = (st_c[:, 1], st_n[:, 1], st_m[:, 1])
    o_ml, _, _ = mlstm_branch(mq, mk, mv, mo, mg, lp, state_fwd, state_bwd)
    return merge_branches(bg, o_na, o_gqa, o_ml, lp)


def moe_swiglu(h, lp):
    b, s, d = h.shape
    t = h.reshape(b * s, d)
    logits = (t @ lp['router_w'] + lp['router_b']).astype(F32)
    top_val, top_idx = lax.top_k(logits, TOP_K)
    top_w = jax.nn.softmax(top_val, axis=-1)
    combine = jnp.einsum('tk,tke->te', top_w, jax.nn.one_hot(top_idx, N_EXPERTS, dtype=F32))
    out = jnp.zeros((b * s, d), F32)
    for e in range(N_EXPERTS):
        gu = t @ lp['w_gate_up'][e] + lp['b_gate_up'][e]
        gate = jnp.minimum(gu[:, :D_FF_EXPERT], SWIGLU_LIMIT)
        up = jnp.clip(gu[:, D_FF_EXPERT:], -SWIGLU_LIMIT, SWIGLU_LIMIT)
        act = (up + 1) * (gate * jax.nn.sigmoid(SWIGLU_ALPHA * gate))
        y = act @ lp['w_down'][e] + lp['b_down'][e]
        out = out + combine[:, e:e + 1] * y.astype(F32)
    return out.astype(h.dtype).reshape(b, s, d)


def sandwich_layer(x, cond, lp, mixer_fn):
    sh1, sc1, g1, sh2, sc2, g2 = adaln(cond, lp['w_mod'], lp['b_mod'])
    h = modulate(rms_norm(x, lp['norm_g'][0]), sh1, sc1)
    mix, extra = mixer_fn(h)
    x = x + g1 * rms_norm(mix, lp['norm_g'][1])
    h = modulate(rms_norm(x, lp['norm_g'][2]), sh2, sc2)
    x = x + g2 * rms_norm(moe_swiglu(h, lp), lp['norm_g'][3])
    return x, extra


def setup_inputs(seed: int = 0) -> dict:
    key = jax.random.key(seed)
    ks = jax.random.split(key, 32)
    nrm = lambda k, shape, sc: sc * jax.random.normal(k, shape, F32)
    d = D_MODEL
    forget_offset = jnp.array([0.0, ML_FORGET_BIAS, 0.0, ML_FORGET_BIAS], F32)[None, :, None]
    return {
        'x_prompt': nrm(ks[0], (BATCH, SEQ, d), 1.0),
        'x_sample': nrm(ks[1], (DEC_BATCH, DEC_SEQ, d), 1.0),
        'cache_na_k': nrm(ks[2], (DEC_BATCH, DEPTH, PAST_LEN, NA_HEADS, HEAD_DIM), 1.0),
        'cache_na_v': nrm(ks[3], (DEC_BATCH, DEPTH, PAST_LEN, NA_HEADS, HEAD_DIM), 1.0),
        'cache_gqa_k': nrm(ks[4], (DEC_BATCH, DEPTH, PAST_LEN, GQA_KV_HEADS, HEAD_DIM), 1.0),
        'cache_gqa_v': nrm(ks[5], (DEC_BATCH, DEPTH, PAST_LEN, GQA_KV_HEADS, HEAD_DIM), 1.0),
        'state_ml_c': nrm(ks[6], (DEC_BATCH, DEPTH, 2, ML_HEADS, ML_DK, ML_DV), 0.5),
        'state_ml_n': nrm(ks[7], (DEC_BATCH, DEPTH, 2, ML_HEADS, ML_DK), 0.5),
        'state_ml_m': nrm(ks[8], (DEC_BATCH, DEPTH, 2, ML_HEADS), 1.0),
        'c': nrm(ks[9], (DEC_BATCH, d), 1.0),
        'c_ctx': nrm(ks[10], (d,), 1.0),
        'w_mod': nrm(ks[11], (DEPTH, d, 6 * d), 0.5 * d ** -0.5),
        'b_mod': nrm(ks[12], (DEPTH, 6 * d), 0.01),
        'norm_g': 1.0 + nrm(ks[13], (DEPTH, 4, d), 0.05),
        'w_in': nrm(ks[14], (DEPTH, d, IN_COLS), d ** -0.5),
        'na_rel_bias': nrm(ks[15], (DEPTH, NA_HEADS, 2 * NA_WIN_ROWS - 1, 2 * NA_WIN_COLS - 1), 0.1),
        'gqa_qk_norm': 1.0 + nrm(ks[16], (DEPTH, 2, HEAD_DIM), 0.05),
        'ml_gate_bias': nrm(ks[17], (DEPTH, 4, ML_HEADS), 0.1) + forget_offset,
        'ml_norm_g': 1.0 + nrm(ks[18], (DEPTH, MLV_W), 0.05),
        'w_branch_na': nrm(ks[19], (DEPTH, NA_W, d), NA_W ** -0.5),
        'w_branch_gqa': nrm(ks[20], (DEPTH, GQ_W, d), GQ_W ** -0.5),
        'w_branch_ml': nrm(ks[21], (DEPTH, MLV_W, d), MLV_W ** -0.5),
        'w_out': nrm(ks[22], (DEPTH, d, d), d ** -0.5),
        'router_w': nrm(ks[23], (DEPTH, d, N_EXPERTS), d ** -0.5),
        'router_b': nrm(ks[24], (DEPTH, N_EXPERTS), 0.01),
        'w_gate_up': nrm(ks[25], (DEPTH, N_EXPERTS, d, 2 * D_FF_EXPERT), d ** -0.5),
        'b_gate_up': nrm(ks[26], (DEPTH, N_EXPERTS, 2 * D_FF_EXPERT), 0.01),
        'w_down': nrm(ks[27], (DEPTH, N_EXPERTS, D_FF_EXPERT, d), D_FF_EXPERT ** -0.5),
        'b_down': nrm(ks[28], (DEPTH, N_EXPERTS, d), 0.01),
    }


def reference(x_prompt, x_sample, cache_na_k, cache_na_v, cache_gqa_k, cache_gqa_v, state_ml_c, state_ml_n,
              state_ml_m, c, c_ctx, w_mod, b_mod, norm_g, w_in, na_rel_bias, gqa_qk_norm, ml_gate_bias, ml_norm_g,
              w_branch_na, w_branch_gqa, w_branch_ml, w_out, router_w, router_b, w_gate_up, b_gate_up, w_down,
              b_down):
    xp = x_prompt
    xs = x_sample
    na_k_l, na_v_l, gk_l, gv_l, mc_l, mn_l, mm_l = [], [], [], [], [], [], []
    for l in range(DEPTH):
        lp = {'w_mod': w_mod[l], 'b_mod': b_mod[l], 'norm_g': norm_g[l], 'w_in': w_in[l],
              'rel_bias': na_rel_bias[l], 'qk_norm': gqa_qk_norm[l], 'gate_bias': ml_gate_bias[l],
              'ml_norm': ml_norm_g[l], 'w_br_na': w_branch_na[l], 'w_br_gqa': w_branch_gqa[l],
              'w_br_ml': w_branch_ml[l], 'w_out': w_out[l], 'router_w': router_w[l], 'router_b': router_b[l],
              'w_gate_up': w_gate_up[l], 'b_gate_up': b_gate_up[l], 'w_down': w_down[l], 'b_down': b_down[l]}
        xp, ctx = sandwich_layer(xp, c_ctx[None, :], lp, lambda hh: mixer_context(hh, lp))
        na_k_l.append(ctx[0]); na_v_l.append(ctx[1]); gk_l.append(ctx[2]); gv_l.append(ctx[3])
        mc_l.append(ctx[4]); mn_l.append(ctx[5]); mm_l.append(ctx[6])
        cache_l = (cache_na_k[:, l], cache_na_v[:, l], cache_gqa_k[:, l], cache_gqa_v[:, l],
                   state_ml_c[:, l], state_ml_n[:, l], state_ml_m[:, l])
        xs, _ = sandwich_layer(xs, c, lp, lambda hh: (mixer_latent(hh, lp, cache_l), None))
    new_na_k = jnp.stack(na_k_l, axis=1)
    new_na_v = jnp.stack(na_v_l, axis=1)
    new_gqa_k = jnp.stack(gk_l, axis=1)
    new_gqa_v = jnp.stack(gv_l, axis=1)
    new_ml_c = jnp.stack(mc_l, axis=1)
    new_ml_n = jnp.stack(mn_l, axis=1)
    new_ml_m = jnp.stack(mm_l, axis=1)
    return (xp, xs, new_na_k, new_na_v, new_gqa_k, new_gqa_v, new_ml_c, new_ml_n, new_ml_m)
```

```python
import functools

import numpy as np
import jax
import jax.numpy as jnp
from jax import lax
from jax.experimental import pallas as pl
from jax.experimental.pallas import tpu as pltpu

D_MODEL = 1024
BATCH = 16
SEQ = 256
DEPTH = 2
DEC_BATCH = 4
DEC_SEQ = 1024
PAST_LEN = 256
GRID_W = 64
GRID_ROWS = DEC_SEQ // GRID_W
HEAD_DIM = 64
NA_HEADS = 4
NA_WIN_ROWS = 8
NA_WIN_COLS = 16
GQA_Q_HEADS = 8
GQA_KV_HEADS = 2
GQA_GROUP = GQA_Q_HEADS // GQA_KV_HEADS
ML_HEADS = 4
ML_DK = 64
ML_DV = 64
ML_CHUNK = 64
N_BRANCH = 3
N_EXPERTS = 32
TOP_K = 4
D_FF_EXPERT = 1024
SWIGLU_LIMIT = 7.0
SWIGLU_ALPHA = 1.702
ROPE_THETA = 10000.0
RMS_EPS = 1e-6
NEG_INF = -1e30

NA_W = NA_HEADS * HEAD_DIM
GQ_W = GQA_Q_HEADS * HEAD_DIM
GKV_W = GQA_KV_HEADS * HEAD_DIM
ML_W = ML_HEADS * ML_DK
ML_GATES = 4 * ML_HEADS
NA_QKV_W = 3 * NA_W
GQA_QKV_W = GQ_W + 2 * GKV_W
ML_QKVO_W = 4 * ML_W
PROJ_W = NA_QKV_W + GQA_QKV_W + ML_QKVO_W
BG_W = N_BRANCH * D_MODEL

N_CTX_TOK = BATCH * SEQ
N_LAT_TOK = DEC_BATCH * DEC_SEQ
N_TOK = N_CTX_TOK + N_LAT_TOK
N_COND = 1 + DEC_BATCH
COND_ROWS = 8
LANE = 128
TOK_TILE = 256
N_TOK_TILES = N_TOK // TOK_TILE
N_CTX_TILES = N_CTX_TOK // TOK_TILE
LAT_TILES_PER_SEQ = DEC_SEQ // TOK_TILE
MOE_TILE = 256
N_SLOTS = N_TOK * TOP_K
MOE_ROWS = N_SLOTS + N_EXPERTS * MOE_TILE
N_MOE_TILES = MOE_ROWS // MOE_TILE
ML_STEP = 2 * ML_CHUNK
VMEM_LIMIT = 56 * 1024 * 1024

F32 = jnp.float32
BF16 = jnp.bfloat16
HIGHEST = lax.Precision.HIGHEST


def _mod_row(i):
    return jnp.where(i < N_CTX_TILES, 0, 1 + (i - N_CTX_TILES) // LAT_TILES_PER_SEQ)


def _rms(x):
    return x * lax.rsqrt(jnp.mean(x * x, axis=-1, keepdims=True) + RMS_EPS)


def _dot(a, b):
    return jnp.dot(a, b, preferred_element_type=F32)


def _dot_nt(a, b):
    return lax.dot_general(a, b, (((1,), (1,)), ((), ())), preferred_element_type=F32)


def _dot_tn(a, b):
    return lax.dot_general(a, b, (((0,), (0,)), ((), ())), preferred_element_type=F32)


def _params(*sem):
    return pltpu.CompilerParams(dimension_semantics=sem, vmem_limit_bytes=VMEM_LIMIT)


ADALN_TN = 512
ADALN_KC = 64


def _adaln_kernel(condt_ref, w_ref, b_ref, o_ref):
    def body(k, accs):
        k0 = pl.multiple_of(k * ADALN_KC, ADALN_KC)
        wc = w_ref[0, pl.ds(k0, ADALN_KC), :]
        sc = condt_ref[pl.ds(k0, ADALN_KC), :]
        sc = sc * jax.nn.sigmoid(sc)
        out = []
        for r in range(N_COND):
            prod = wc * sc[:, r:r + 1]
            out.append(accs[r] + jnp.sum(prod.reshape(ADALN_KC // 8, 8, ADALN_TN), axis=0))
        return tuple(out)

    accs = lax.fori_loop(0, D_MODEL // ADALN_KC, body,
                         tuple(jnp.zeros((8, ADALN_TN), F32) for _ in range(N_COND)))
    rows = [jnp.sum(a, axis=0, keepdims=True) for a in accs]
    rows += [jnp.zeros((1, ADALN_TN), F32)] * (COND_ROWS - N_COND)
    o_ref[0] = jnp.concatenate(rows, axis=0) + b_ref[0]


def _adaln(condt, w_mod, b_mod):
    n = 6 * D_MODEL
    return pl.pallas_call(
        _adaln_kernel,
        out_shape=jax.ShapeDtypeStruct((DEPTH, COND_ROWS, n), F32),
        grid=(DEPTH, n // ADALN_TN),
        in_specs=[pl.BlockSpec((D_MODEL, COND_ROWS), lambda l, j: (0, 0)),
                  pl.BlockSpec((1, D_MODEL, ADALN_TN), lambda l, j: (l, 0, j)),
                  pl.BlockSpec((1, 1, ADALN_TN), lambda l, j: (l, 0, j))],
        out_specs=pl.BlockSpec((1, COND_ROWS, ADALN_TN), lambda l, j: (l, 0, j)),
        compiler_params=_params("arbitrary", "arbitrary"),
        name="adaln",
    )(condt, w_mod, b_mod.reshape(DEPTH, 1, n))


def _inproj_kernel(x_ref, mod_ref, ng_ref, wa_ref, wg_ref, wgt_ref, gb_ref, gbt_ref,
                   na_ref, gqa_ref, ml_ref, g_ref, gt_ref):
    x = x_ref[...]
    h = _rms(x) * ng_ref[0:1, :]
    h = h * (1.0 + mod_ref[1:2, :]) + mod_ref[0:1, :]
    y = _dot(h.astype(BF16), wa_ref[...])
    na_ref[...] = y[:, :NA_QKV_W]
    gqa_ref[...] = y[:, NA_QKV_W:NA_QKV_W + GQA_QKV_W]
    ml_ref[...] = y[:, NA_QKV_W + GQA_QKV_W:]
    g_ref[...] = jnp.dot(h, wg_ref[...], precision=HIGHEST, preferred_element_type=F32) + gb_ref[...]
    gt_ref[...] = lax.dot_general(wgt_ref[...], h, (((1,), (1,)), ((), ())), precision=HIGHEST,
                                  preferred_element_type=F32) + gbt_ref[...]


def _inproj(x, mod, norm_g, wa, wg, wgt, gb, gbt):
    tile = lambda w: pl.BlockSpec((TOK_TILE, w), lambda i: (i, 0))
    full = lambda a: pl.BlockSpec(a.shape, lambda i: (0,) * a.ndim)
    return pl.pallas_call(
        _inproj_kernel,
        out_shape=(jax.ShapeDtypeStruct((N_TOK, NA_QKV_W), F32),
                   jax.ShapeDtypeStruct((N_TOK, GQA_QKV_W), F32),
                   jax.ShapeDtypeStruct((N_TOK, ML_QKVO_W), F32),
                   jax.ShapeDtypeStruct((N_TOK, LANE), F32),
                   jax.ShapeDtypeStruct((ML_GATES, N_TOK), F32)),
        grid=(N_TOK_TILES,),
        in_specs=[tile(D_MODEL),
                  pl.BlockSpec((None, 8, D_MODEL), lambda i: (_mod_row(i), 0, 0)),
                  full(norm_g), full(wa), full(wg), full(wgt), full(gb), full(gbt)],
        out_specs=(tile(NA_QKV_W), tile(GQA_QKV_W), tile(ML_QKVO_W), tile(LANE),
                   pl.BlockSpec((ML_GATES, TOK_TILE), lambda i: (0, i))),
        compiler_params=_params("arbitrary"),
        name="inproj",
    )(x, mod, norm_g, wa, wg, wgt, gb, gbt)


def _head(x, h):
    return x[:, h * HEAD_DIM:(h + 1) * HEAD_DIM]


def _head_rms(x, n_heads, g):
    return jnp.concatenate([_rms(_head(x, h)) * g for h in range(n_heads)], axis=-1)


def _softmax_pv(score_blocks, value_blocks):
    m = score_blocks[0].max(axis=-1, keepdims=True)
    for s in score_blocks[1:]:
        m = jnp.maximum(m, s.max(axis=-1, keepdims=True))
    ps = [jnp.exp(s - m) for s in score_blocks]
    l = ps[0].sum(axis=-1, keepdims=True)
    for p in ps[1:]:
        l = l + p.sum(axis=-1, keepdims=True)
    inv = 1.0 / l
    o = None
    for p, v in zip(ps, value_blocks):
        t = _dot((p * inv).astype(BF16), v.astype(BF16))
        o = t if o is None else o + t
    return o


_ATT_SCALE = HEAD_DIM ** -0.5


def _ctx_attn_kernel(na_ref, gqa_ref, qkn_ref, o_ref, kn_ref):
    na = na_ref[...]
    outs = []
    for h in range(NA_HEADS):
        q = _head(na, h) * _ATT_SCALE
        k = _head(na, NA_HEADS + h)
        v = _head(na, 2 * NA_HEADS + h)
        s = _dot_nt(q.astype(BF16), k.astype(BF16))
        outs.append(_softmax_pv([s], [v]))
    gq = gqa_ref[:, :GQ_W]
    gk = gqa_ref[:, GQ_W:GQ_W + GKV_W]
    gv = gqa_ref[:, GQ_W + GKV_W:]
    qn = _head_rms(gq, GQA_Q_HEADS, qkn_ref[0:1, :])
    kn = _head_rms(gk, GQA_KV_HEADS, qkn_ref[1:2, :])
    kn_ref[...] = kn
    for h in range(GQA_Q_HEADS):
        kv = h // GQA_GROUP
        s = _dot_nt((_head(qn, h) * _ATT_SCALE).astype(BF16), _head(kn, kv).astype(BF16))
        outs.append(_softmax_pv([s], [_head(gv, kv)]))
    o_ref[...] = jnp.concatenate(outs, axis=-1)


def _ctx_attn(na, gqa, qk_norm):
    return pl.pallas_call(
        _ctx_attn_kernel,
        out_shape=(jax.ShapeDtypeStruct((N_CTX_TOK, NA_W + GQ_W), F32),
                   jax.ShapeDtypeStruct((N_CTX_TOK, GKV_W), F32)),
        grid=(BATCH,),
        in_specs=[pl.BlockSpec((SEQ, NA_QKV_W), lambda b: (b, 0)),
                  pl.BlockSpec((SEQ, GQA_QKV_W), lambda b: (b, 0)),
                  pl.BlockSpec((2, HEAD_DIM), lambda b: (0, 0))],
        out_specs=(pl.BlockSpec((SEQ, NA_W + GQ_W), lambda b: (b, 0)),
                   pl.BlockSpec((SEQ, GKV_W), lambda b: (b, 0))),
        compiler_params=_params("arbitrary"),
        name="ctx_attn",
    )(na, gqa, qk_norm)


def _lat_na_kernel(q_ref, kv_ref, ck_ref, cv_ref, tab_ref, o_ref):
    r = pl.program_id(1)
    row_start = jnp.clip(r - NA_WIN_ROWS // 2, 0, GRID_ROWS - NA_WIN_ROWS)
    k0 = pl.multiple_of(row_start * GRID_W, GRID_W)
    n_loc = NA_WIN_ROWS * GRID_W
    kwin = kv_ref[pl.ds(k0, n_loc), NA_W:2 * NA_W]
    vwin = kv_ref[pl.ds(k0, n_loc), 2 * NA_W:3 * NA_W]
    q = q_ref[:, :NA_W]
    ck = ck_ref[...]
    cv = cv_ref[...]
    outs = []
    for h in range(NA_HEADS):
        qh = (_head(q, h) * _ATT_SCALE).astype(BF16)
        s_loc = _dot_nt(qh, _head(kwin, h).astype(BF16)) + tab_ref[h, 0]
        s_ctx = _dot_nt(qh, _head(ck, h).astype(BF16))
        outs.append(_softmax_pv([s_loc, s_ctx], [_head(vwin, h), _head(cv, h)]))
    o_ref[...] = jnp.concatenate(outs, axis=-1)


def _na_variant(r):
    return r - jnp.clip(r - NA_WIN_ROWS // 2, 0, GRID_ROWS - NA_WIN_ROWS)


def _lat_na(na, ck, cv, tab):
    ctx_rows = N_CTX_TOK // GRID_W
    ctx_seqs = N_CTX_TOK // DEC_SEQ
    return pl.pallas_call(
        _lat_na_kernel,
        out_shape=jax.ShapeDtypeStruct((N_LAT_TOK, NA_W), F32),
        grid=(DEC_BATCH, GRID_ROWS),
        in_specs=[pl.BlockSpec((GRID_W, NA_QKV_W), lambda b, r: (ctx_rows + b * GRID_ROWS + r, 0)),
                  pl.BlockSpec((DEC_SEQ, NA_QKV_W), lambda b, r: (ctx_seqs + b, 0)),
                  pl.BlockSpec((None, PAST_LEN, NA_W), lambda b, r: (b, 0, 0)),
                  pl.BlockSpec((None, PAST_LEN, NA_W), lambda b, r: (b, 0, 0)),
                  pl.BlockSpec((NA_HEADS, 1, GRID_W, NA_WIN_ROWS * GRID_W),
                               lambda b, r: (0, _na_variant(r), 0, 0))],
        out_specs=pl.BlockSpec((GRID_W, NA_W), lambda b, r: (b * GRID_ROWS + r, 0)),
        compiler_params=_params("arbitrary", "arbitrary"),
        name="lat_na",
    )(na, na, ck, cv, tab)


LAT_GQA_TQ = 256


def _rope(x, cos, sin):
    w = x.shape[-1]
    lane = lax.broadcasted_iota(jnp.int32, x.shape, 1)
    first = (lane % 32) < 16
    partner = jnp.where(first, pltpu.roll(x, w - 16, 1), pltpu.roll(x, 16, 1))
    return x * cos + partner * sin


def _lat_gqa_kernel(q_ref, kv_ref, ck_ref, cv_ref, qkn_ref, cq_ref, sq_ref, ckk_ref, skk_ref, o_ref):
    gq = q_ref[:, :GQ_W]
    gk = kv_ref[:, GQ_W:GQ_W + GKV_W]
    gv = kv_ref[:, GQ_W + GKV_W:]
    qn = _rope(_head_rms(gq, GQA_Q_HEADS, qkn_ref[0:1, :]), cq_ref[...], sq_ref[...])
    kn = _rope(_head_rms(gk, GQA_KV_HEADS, qkn_ref[1:2, :]), ckk_ref[...], skk_ref[...])
    ck = ck_ref[...]
    cv = cv_ref[...]
    outs = []
    for h in range(GQA_Q_HEADS):
        kv = h // GQA_GROUP
        qh = (_head(qn, h) * _ATT_SCALE).astype(BF16)
        s_loc = _dot_nt(qh, _head(kn, kv).astype(BF16))
        s_ctx = _dot_nt(qh, _head(ck, kv).astype(BF16))
        outs.append(_softmax_pv([s_loc, s_ctx], [_head(gv, kv), _head(cv, kv)]))
    o_ref[...] = jnp.concatenate(outs, axis=-1)


def _lat_gqa(gqa, ck, cv, qk_norm, cos_q, sin_q, cos_k, sin_k):
    nq = DEC_SEQ // LAT_GQA_TQ
    ctx_qblocks = N_CTX_TOK // LAT_GQA_TQ
    ctx_seqs = N_CTX_TOK // DEC_SEQ
    full = lambda a: pl.BlockSpec(a.shape, lambda b, j: (0,) * a.ndim)
    return pl.pallas_call(
        _lat_gqa_kernel,
        out_shape=jax.ShapeDtypeStruct((N_LAT_TOK, GQ_W), F32),
        grid=(DEC_BATCH, nq),
        in_specs=[pl.BlockSpec((LAT_GQA_TQ, GQA_QKV_W), lambda b, j: (ctx_qblocks + b * nq + j, 0)),
                  pl.BlockSpec((DEC_SEQ, GQA_QKV_W), lambda b, j: (ctx_seqs + b, 0)),
                  pl.BlockSpec((None, PAST_LEN, GKV_W), lambda b, j: (b, 0, 0)),
                  pl.BlockSpec((None, PAST_LEN, GKV_W), lambda b, j: (b, 0, 0)),
                  full(qk_norm),
                  pl.BlockSpec((LAT_GQA_TQ, GQ_W), lambda b, j: (j, 0)),
                  pl.BlockSpec((LAT_GQA_TQ, GQ_W), lambda b, j: (j, 0)),
                  full(cos_k), full(sin_k)],
        out_specs=pl.BlockSpec((LAT_GQA_TQ, GQ_W), lambda b, j: (b * nq + j, 0)),
        compiler_params=_params("arbitrary", "arbitrary"),
        name="lat_gqa",
    )(gqa, gqa, ck, cv, qk_norm, cos_q, sin_q, cos_k, sin_k)


def _mlstm_chunk(q, k, v, i_col, f_col, i_row, f_row, c, n, m, backward):
    L = ML_CHUNK
    lf_col = jax.nn.log_sigmoid(f_col)
    lf_row = jax.nn.log_sigmoid(f_row)
    si = lax.broadcasted_iota(jnp.int32, (L, L), 0)
    ui = lax.broadcasted_iota(jnp.int32, (L, L), 1)
    mask = (ui >= si) if backward else (ui <= si)
    mask_t = (si >= ui) if backward else (si <= ui)
    bcum_col = jnp.sum(jnp.where(mask, lf_row, 0.0), axis=1, keepdims=True)
    bcum_row = jnp.sum(jnp.where(mask_t, lf_col, 0.0), axis=0, keepdims=True)
    b_last = jnp.sum(lf_col, axis=0, keepdims=True)
    log_d = jnp.where(mask, bcum_col - bcum_row + i_row, NEG_INF)
    m_inter = bcum_col + m
    m_s = jnp.maximum(m_inter, jnp.max(log_d, axis=1, keepdims=True))
    w_intra = jnp.exp(log_d - m_s)
    w_inter = jnp.exp(m_inter - m_s)
    ks = k * (ML_DK ** -0.5)
    qb = q.astype(BF16)
    qk = _dot_nt(qb, ks.astype(BF16)) * w_intra
    num = _dot(qk.astype(BF16), v.astype(BF16)) + w_inter * _dot(qb, c.astype(BF16))
    den = jnp.sum(qk, axis=1, keepdims=True) + w_inter * jnp.sum(q * n, axis=1, keepdims=True)
    h_out = num / jnp.maximum(jnp.abs(den), jnp.exp(-m_s))
    log_w = b_last - bcum_col + i_col
    m_new = jnp.maximum(b_last + m, jnp.max(log_w, axis=0, keepdims=True))
    w_state = jnp.exp(log_w - m_new)
    decay = jnp.exp(b_last + m - m_new)
    kw = ks * w_state
    c_new = decay * c + _dot_tn(kw.astype(BF16), v.astype(BF16))
    n_new = decay * n + jnp.sum(kw, axis=0, keepdims=True)
    return h_out, c_new, n_new, m_new


def _mlstm_kernel(qf_ref, gf_ref, gtf_ref, qb_ref, gb_ref, gtb_ref, c0_ref, n0_ref, m0_ref,
                  hf_ref, hb_ref, c_out, n_out, m_out, c_s, n_s, m_s):
    j = pl.program_id(1)
    nj = pl.num_programs(1)

    @pl.when(j == 0)
    def _():
        c_s[...] = c0_ref[0]
        n_s[...] = n0_ref[0]
        m_s[...] = m0_ref[0]

    n_chunks = ML_STEP // ML_CHUNK
    for d, (x_ref, g_ref, gt_ref, h_ref) in enumerate(((qf_ref, gf_ref, gtf_ref, hf_ref),
                                                       (qb_ref, gb_ref, gtb_ref, hb_ref))):
        backward = d == 1
        for h in range(ML_HEADS):
            c = c_s[d, h]
            n = n_s[d, h]
            m = m_s[d, h]
            for cc in (range(n_chunks - 1, -1, -1) if backward else range(n_chunks)):
                rows = slice(cc * ML_CHUNK, (cc + 1) * ML_CHUNK)
                q = x_ref[rows, h * ML_DK:(h + 1) * ML_DK]
                k = x_ref[rows, ML_W + h * ML_DK:ML_W + (h + 1) * ML_DK]
                v = x_ref[rows, 2 * ML_W + h * ML_DV:2 * ML_W + (h + 1) * ML_DV]
                gi = 2 * d * ML_HEADS + h
                gf = gi + ML_HEADS
                h_out, c, n, m = _mlstm_chunk(
                    q, k, v, g_ref[rows, gi:gi + 1], g_ref[rows, gf:gf + 1],
                    gt_ref[gi:gi + 1, rows], gt_ref[gf:gf + 1, rows], c, n, m, backward)
                h_ref[rows, h * ML_DV:(h + 1) * ML_DV] = h_out
            c_s[d, h] = c
            n_s[d, h] = n
            m_s[d, h] = m

    @pl.when(j == nj - 1)
    def _():
        c_out[0] = c_s[...]
        n_out[0] = n_s[...]
        m_out[0] = m_s[...]


def _mlstm(ml, g, gt, c0, n0, m0, n_seq, seq_len, tok0):
    nj = seq_len // ML_STEP
    b0 = tok0 // ML_STEP
    fwd = lambda s, j: (b0 + s * nj + j, 0)
    bwd = lambda s, j: (b0 + s * nj + nj - 1 - j, 0)
    fwd_t = lambda s, j: (0, b0 + s * nj + j)
    bwd_t = lambda s, j: (0, b0 + s * nj + nj - 1 - j)
    state = lambda a: pl.BlockSpec((1,) + a.shape[1:], lambda s, j: (s,) + (0,) * (a.ndim - 1))
    n_tok = n_seq * seq_len
    out_f = lambda s, j: (s * nj + j, 0)
    out_b = lambda s, j: (s * nj + nj - 1 - j, 0)
    return pl.pallas_call(
        _mlstm_kernel,
        out_shape=(jax.ShapeDtypeStruct((n_tok, ML_W), F32),
                   jax.ShapeDtypeStruct((n_tok, ML_W), F32),
                   jax.ShapeDtypeStruct(c0.shape, F32),
                   jax.ShapeDtypeStruct(n0.shape, F32),
                   jax.ShapeDtypeStruct(m0.shape, F32)),
        grid=(n_seq, nj),
        in_specs=[pl.BlockSpec((ML_STEP, ML_QKVO_W), fwd),
                  pl.BlockSpec((ML_STEP, LANE), fwd),
                  pl.BlockSpec((ML_GATES, ML_STEP), fwd_t),
                  pl.BlockSpec((ML_STEP, ML_QKVO_W), bwd),
                  pl.BlockSpec((ML_STEP, LANE), bwd),
                  pl.BlockSpec((ML_GATES, ML_STEP), bwd_t),
                  state(c0), state(n0), state(m0)],
        out_specs=(pl.BlockSpec((ML_STEP, ML_W), out_f),
                   pl.BlockSpec((ML_STEP, ML_W), out_b),
                   state(c0), state(n0), state(m0)),
        scratch_shapes=[pltpu.VMEM(c0.shape[1:], F32), pltpu.VMEM(n0.shape[1:], F32),
                        pltpu.VMEM(m0.shape[1:], F32)],
        compiler_params=_params("arbitrary", "arbitrary"),
        name="mlstm",
    )(ml, g, gt, ml, g, gt, c0, n0, m0)


def _merge_kernel(x_ref, mod_ref, ng_ref, o_ref, hf_ref, hb_ref, mo_ref, mln_ref,
                  wbg_ref, wna_ref, wgqa_ref, wml_ref, wout_ref, rw_ref, rb_ref,
                  x1_ref, h2_ref, idx_ref, tw_ref):
    x = x_ref[...]
    h = _rms(x) * ng_ref[0:1, :]
    h = (h * (1.0 + mod_ref[1:2, :]) + mod_ref[0:1, :]).astype(BF16)
    hs = hf_ref[...] + hb_ref[...]
    hn = jnp.concatenate([_rms(_head(hs, hd)) for hd in range(ML_HEADS)], axis=-1) * mln_ref[...]
    o_ml = hn * jax.nn.sigmoid(mo_ref[...])
    branches = ((o_ref[:, :NA_W], wna_ref), (o_ref[:, NA_W:], wgqa_ref), (o_ml, wml_ref))
    z = None
    for b, (o_b, w_b) in enumerate(branches):
        gate = jax.nn.sigmoid(_dot(h, wbg_ref[:, b * D_MODEL:(b + 1) * D_MODEL]))
        t = gate * _dot(o_b.astype(BF16), w_b[...])
        z = t if z is None else z + t
    mix = _dot(z.astype(BF16), wout_ref[...])
    x1 = x + mod_ref[2:3, :] * (_rms(mix) * ng_ref[1:2, :])
    x1_ref[...] = x1
    h2 = _rms(x1) * ng_ref[2:3, :]
    h2 = h2 * (1.0 + mod_ref[4:5, :]) + mod_ref[3:4, :]
    h2_ref[...] = h2.astype(BF16)
    logits = jnp.dot(h2, rw_ref[...], precision=HIGHEST, preferred_element_type=F32) + rb_ref[...]
    lane = lax.broadcasted_iota(jnp.int32, logits.shape, 1)
    lane_f = lane.astype(F32)
    idx_out = jnp.zeros(logits.shape, jnp.int32)
    val_out = jnp.zeros(logits.shape, F32)
    v0 = None
    denom = None
    for k in range(TOP_K):
        mx = jnp.max(logits, axis=-1, keepdims=True)
        sel = jnp.min(jnp.where(logits == mx, lane_f, float(LANE)), axis=-1, keepdims=True).astype(jnp.int32)
        if k == 0:
            v0 = mx
        e = jnp.exp(mx - v0)
        denom = e if denom is None else denom + e
        idx_out = jnp.where(lane == k, sel, idx_out)
        val_out = jnp.where(lane == k, e, val_out)
        logits = jnp.where(lane == sel, NEG_INF, logits)
    idx_ref[...] = idx_out
    tw_ref[...] = val_out / denom


def _merge(x, mod, norm_g, o_attn, hf, hb, ml, ml_norm, wbg, wna, wgqa, wml, wout, rw, rb):
    tile = lambda w: pl.BlockSpec((TOK_TILE, w), lambda i: (i, 0))
    full = lambda a: pl.BlockSpec(a.shape, lambda i: (0,) * a.ndim)
    return pl.pallas_call(
        _merge_kernel,
        out_shape=(jax.ShapeDtypeStruct((N_TOK, D_MODEL), F32),
                   jax.ShapeDtypeStruct((N_TOK, D_MODEL), BF16),
                   jax.ShapeDtypeStruct((N_TOK, LANE), jnp.int32),
                   jax.ShapeDtypeStruct((N_TOK, LANE), F32)),
        grid=(N_TOK_TILES,),
        in_specs=[tile(D_MODEL),
                  pl.BlockSpec((None, 8, D_MODEL), lambda i: (_mod_row(i), 0, 0)),
                  full(norm_g), tile(NA_W + GQ_W), tile(ML_W), tile(ML_W),
                  pl.BlockSpec((TOK_TILE, ML_W), lambda i: (i, 3)),
                  full(ml_norm), full(wbg), full(wna), full(wgqa), full(wml), full(wout),
                  full(rw), full(rb)],
        out_specs=(tile(D_MODEL), tile(D_MODEL), tile(LANE), tile(LANE)),
        compiler_params=_params("arbitrary"),
        name="merge",
    )(x, mod, norm_g, o_attn, hf, hb, ml, ml_norm, wbg, wna, wgqa, wml, wout, rw, rb)


MOE_CAST_ROWS = 128


def _moe_kernel(te_ref, nv_ref, x_ref, rw_ref, wgu_ref, bgu_ref, wd_ref, bd_ref, o_ref, wgu_s, wd_s):
    i = pl.program_id(0)
    prev = te_ref[jnp.maximum(i - 1, 0)]
    new_expert = jnp.logical_or(i == 0, te_ref[i] != prev)

    @pl.when(new_expert)
    def _():
        def cast(r, carry):
            r0 = pl.multiple_of(r * MOE_CAST_ROWS, MOE_CAST_ROWS)
            wgu_s[pl.ds(r0, MOE_CAST_ROWS), :] = wgu_ref[0, pl.ds(r0, MOE_CAST_ROWS), :].astype(BF16)
            wd_s[pl.ds(r0, MOE_CAST_ROWS), :] = wd_ref[0, pl.ds(r0, MOE_CAST_ROWS), :].astype(BF16)
            return carry
        lax.fori_loop(0, D_MODEL // MOE_CAST_ROWS, cast, 0)

    @pl.when(i < nv_ref[0])
    def _():
        gu = _dot(x_ref[...], wgu_s[...]) + bgu_ref[0]
        gate = jnp.minimum(gu[:, :D_FF_EXPERT], SWIGLU_LIMIT)
        up = jnp.clip(gu[:, D_FF_EXPERT:], -SWIGLU_LIMIT, SWIGLU_LIMIT)
        act = (up + 1.0) * (gate * jax.nn.sigmoid(SWIGLU_ALPHA * gate))
        y = _dot(act.astype(BF16), wd_s[...]) + bd_ref[0]
        o_ref[...] = y * rw_ref[...]

    @pl.when(i >= nv_ref[0])
    def _():
        o_ref[...] = jnp.zeros(o_ref.shape, F32)


def _moe(tile_expert, n_valid, xs, row_w, wgu, bgu, wd, bd):
    assert D_FF_EXPERT == D_MODEL
    grid_spec = pltpu.PrefetchScalarGridSpec(
        num_scalar_prefetch=2,
        grid=(N_MOE_TILES,),
        in_specs=[pl.BlockSpec((MOE_TILE, D_MODEL), lambda i, te, nv: (i, 0)),
                  pl.BlockSpec((MOE_TILE, 1), lambda i, te, nv: (i, 0)),
                  pl.BlockSpec((1, D_MODEL, 2 * D_FF_EXPERT), lambda i, te, nv: (te[i], 0, 0)),
                  pl.BlockSpec((1, 1, 2 * D_FF_EXPERT), lambda i, te, nv: (te[i], 0, 0)),
                  pl.BlockSpec((1, D_FF_EXPERT, D_MODEL), lambda i, te, nv: (te[i], 0, 0)),
                  pl.BlockSpec((1, 1, D_MODEL), lambda i, te, nv: (te[i], 0, 0))],
        out_specs=pl.BlockSpec((MOE_TILE, D_MODEL), lambda i, te, nv: (i, 0)),
        scratch_shapes=[pltpu.VMEM((D_MODEL, 2 * D_FF_EXPERT), BF16),
                        pltpu.VMEM((D_FF_EXPERT, D_MODEL), BF16)],
    )
    return pl.pallas_call(
        _moe_kernel,
        out_shape=jax.ShapeDtypeStruct((MOE_ROWS, D_MODEL), F32),
        grid_spec=grid_spec,
        compiler_params=_params("arbitrary"),
        name="moe",
    )(tile_expert, n_valid, xs, row_w, wgu, bgu.reshape(N_EXPERTS, 1, -1), wd, bd.reshape(N_EXPERTS, 1, -1))


def _route(top_idx, top_w):
    e_flat = top_idx.reshape(N_SLOTS)
    onehot = (e_flat[:, None] == jnp.arange(N_EXPERTS, dtype=jnp.int32)[None, :]).astype(jnp.int32)
    csum = jnp.cumsum(onehot, axis=0)
    rank = jnp.take_along_axis(csum, e_flat[:, None], axis=1)[:, 0] - 1
    counts = csum[-1]
    padded = ((counts + MOE_TILE - 1) // MOE_TILE) * MOE_TILE
    ends = jnp.cumsum(padded)
    starts = ends - padded
    slot_row = starts[e_flat] + rank
    row_token = jnp.zeros((MOE_ROWS,), jnp.int32).at[slot_row].set(
        jnp.arange(N_SLOTS, dtype=jnp.int32) // TOP_K)
    row_w = jnp.zeros((MOE_ROWS,), F32).at[slot_row].set(top_w.reshape(N_SLOTS))
    n_valid = (ends[-1] // MOE_TILE).astype(jnp.int32)
    tile_start = jnp.arange(N_MOE_TILES, dtype=jnp.int32) * MOE_TILE
    tile_expert = jnp.sum((tile_start[:, None] >= ends[None, :]).astype(jnp.int32), axis=1)
    last_expert = tile_expert[jnp.maximum(n_valid - 1, 0)]
    tile_expert = jnp.where(jnp.arange(N_MOE_TILES) < n_valid, tile_expert, last_expert)
    return row_token, row_w, slot_row, tile_expert.astype(jnp.int32), n_valid.reshape(1)


def _final_kernel(x1_ref, moe_ref, mod_ref, ng_ref, o_ref):
    o_ref[...] = x1_ref[...] + mod_ref[5:6, :] * (_rms(moe_ref[...]) * ng_ref[3:4, :])


def _final(x1, moe, mod, norm_g):
    tile = pl.BlockSpec((TOK_TILE, D_MODEL), lambda i: (i, 0))
    return pl.pallas_call(
        _final_kernel,
        out_shape=jax.ShapeDtypeStruct((N_TOK, D_MODEL), F32),
        grid=(N_TOK_TILES,),
        in_specs=[tile, tile,
                  pl.BlockSpec((None, 8, D_MODEL), lambda i: (_mod_row(i), 0, 0)),
                  pl.BlockSpec(norm_g.shape, lambda i: (0, 0))],
        out_specs=tile,
        compiler_params=_params("arbitrary"),
        name="final",
    )(x1, moe, mod, norm_g)


def _rope_tables():
    nf = HEAD_DIM // 4
    t = np.arange(DEC_SEQ)
    pos = np.stack([t // GRID_W, t % GRID_W], axis=-1).astype(np.float32)
    inv_freq = (np.float32(ROPE_THETA) ** (-np.arange(nf, dtype=np.float32) / nf)).astype(np.float32)
    ang = pos[:, :, None] * inv_freq
    cos = np.cos(ang.astype(np.float64))
    sin = np.sin(ang.astype(np.float64))
    cos_h = np.concatenate([cos[:, 0], cos[:, 0], cos[:, 1], cos[:, 1]], axis=-1)
    sin_h = np.concatenate([-sin[:, 0], sin[:, 0], -sin[:, 1], sin[:, 1]], axis=-1)
    tile = lambda a, n: jnp.asarray(np.tile(a, (1, n)), F32)
    return (tile(cos_h, GQA_Q_HEADS), tile(sin_h, GQA_Q_HEADS),
            tile(cos_h, GQA_KV_HEADS), tile(sin_h, GQA_KV_HEADS))


def _na_bias_table(rel_bias):
    v = np.arange(NA_WIN_ROWS)[:, None, None, None]
    i = np.arange(NA_WIN_ROWS)[None, None, :, None]
    qc = np.arange(GRID_W)[None, :, None, None]
    kc = np.arange(GRID_W)[None, None, None, :]
    dr = np.broadcast_to(i - v + NA_WIN_ROWS - 1, (NA_WIN_ROWS, GRID_W, NA_WIN_ROWS, GRID_W))
    dc = np.broadcast_to(np.clip(kc - qc + NA_WIN_COLS - 1, 0, 2 * NA_WIN_COLS - 2), dr.shape)
    cs = np.clip(qc - NA_WIN_COLS // 2, 0, GRID_W - NA_WIN_COLS)
    valid = np.broadcast_to((kc >= cs) & (kc < cs + NA_WIN_COLS), dr.shape)
    tab = rel_bias.astype(F32)[:, dr, dc]
    tab = jnp.where(valid[None], tab, NEG_INF)
    return tab.reshape(NA_HEADS, NA_WIN_ROWS, GRID_W, NA_WIN_ROWS * GRID_W)


def kernel(x_prompt, x_sample, cache_na_k, cache_na_v, cache_gqa_k, cache_gqa_v, state_ml_c, state_ml_n,
           state_ml_m, c, c_ctx, w_mod, b_mod, norm_g, w_in, na_rel_bias, gqa_qk_norm, ml_gate_bias,
           ml_norm_g, w_branch_na, w_branch_gqa, w_branch_ml, w_out, router_w, router_b, w_gate_up,
           b_gate_up, w_down, b_down):
    x = jnp.concatenate([x_prompt.reshape(N_CTX_TOK, D_MODEL), x_sample.reshape(N_LAT_TOK, D_MODEL)], axis=0)
    cond = jnp.concatenate([c_ctx[None, :], c, jnp.zeros((COND_ROWS - N_COND, D_MODEL), F32)], axis=0)
    mod_all = _adaln(cond.T, w_mod, b_mod).reshape(DEPTH, COND_ROWS, 6, D_MODEL)
    mod_all = jnp.pad(mod_all, ((0, 0), (0, 0), (0, 2), (0, 0)))
    cos_q, sin_q, cos_k, sin_k = _rope_tables()

    ctx_out = [[] for _ in range(7)]
    for l in range(DEPTH):
        mod = mod_all[l]
        wa = w_in[l][:, :PROJ_W].astype(BF16)
        wg = jnp.pad(w_in[l][:, PROJ_W:PROJ_W + ML_GATES], ((0, 0), (0, LANE - ML_GATES)))
        wgt = w_in[l][:, PROJ_W:PROJ_W + ML_GATES].T
        gbias = ml_gate_bias[l].reshape(1, ML_GATES)
        gb = jnp.pad(gbias, ((0, 0), (0, LANE - ML_GATES)))
        wbg = w_in[l][:, PROJ_W + ML_GATES:].astype(BF16)
        na, gqa, ml, g, gt = _inproj(x, mod, norm_g[l], wa, wg, wgt, gb, gbias.T)

        o_ctx, kn_ctx = _ctx_attn(na, gqa, gqa_qk_norm[l])
        tab = _na_bias_table(na_rel_bias[l])
        o_na = _lat_na(na, cache_na_k[:, l].reshape(DEC_BATCH, PAST_LEN, NA_W),
                       cache_na_v[:, l].reshape(DEC_BATCH, PAST_LEN, NA_W), tab)
        o_gqa = _lat_gqa(gqa, cache_gqa_k[:, l].reshape(DEC_BATCH, PAST_LEN, GKV_W),
                         cache_gqa_v[:, l].reshape(DEC_BATCH, PAST_LEN, GKV_W), gqa_qk_norm[l],
                         cos_q, sin_q, cos_k, sin_k)
        o_attn = jnp.concatenate([o_ctx, jnp.concatenate([o_na, o_gqa], axis=1)], axis=0)
        zc = jnp.zeros((BATCH, 2, ML_HEADS, ML_DK, ML_DV), F32)
        zn = jnp.zeros((BATCH, 2, ML_HEADS, 1, ML_DK), F32)
        zm = jnp.zeros((BATCH, 2, ML_HEADS, 1, 1), F32)
        hf_c, hb_c, c_new, n_new, m_new = _mlstm(ml, g, gt, zc, zn, zm, BATCH, SEQ, 0)
        hf_l, hb_l, _, _, _ = _mlstm(ml, g, gt, state_ml_c[:, l],
                                     state_ml_n[:, l].reshape(DEC_BATCH, 2, ML_HEADS, 1, ML_DK),
                                     state_ml_m[:, l].reshape(DEC_BATCH, 2, ML_HEADS, 1, 1),
                                     DEC_BATCH, DEC_SEQ, N_CTX_TOK)
        hf = jnp.concatenate([hf_c, hf_l], axis=0)
        hb = jnp.concatenate([hb_c, hb_l], axis=0)

        rw = jnp.pad(router_w[l], ((0, 0), (0, LANE - N_EXPERTS)))
        rb = jnp.pad(router_b[l].reshape(1, N_EXPERTS), ((0, 0), (0, LANE - N_EXPERTS)),
                     constant_values=NEG_INF)
        x1, h2, top_idx, top_w = _merge(
            x, mod, norm_g[l], o_attn, hf, hb, ml, ml_norm_g[l].reshape(1, ML_W), wbg,
            w_branch_na[l].astype(BF16), w_branch_gqa[l].astype(BF16), w_branch_ml[l].astype(BF16),
            w_out[l].astype(BF16), rw, rb)

        row_token, row_w, slot_row, tile_expert, n_valid = _route(top_idx[:, :TOP_K], top_w[:, :TOP_K])
        xs = jnp.take(h2, row_token, axis=0)
        y = _moe(tile_expert, n_valid, xs, row_w.reshape(MOE_ROWS, 1), w_gate_up[l], b_gate_up[l],
                 w_down[l], b_down[l])
        moe = jnp.take(y, slot_row, axis=0).reshape(N_TOK, TOP_K, D_MODEL).sum(axis=1)
        x = _final(x1, moe, mod, norm_g[l])

        ctx_out[0].append(na[:N_CTX_TOK, NA_W:2 * NA_W].reshape(BATCH, SEQ, NA_HEADS, HEAD_DIM))
        ctx_out[1].append(na[:N_CTX_TOK, 2 * NA_W:].reshape(BATCH, SEQ, NA_HEADS, HEAD_DIM))
        ctx_out[2].append(kn_ctx.reshape(BATCH, SEQ, GQA_KV_HEADS, HEAD_DIM))
        ctx_out[3].append(gqa[:N_CTX_TOK, GQ_W + GKV_W:].reshape(BATCH, SEQ, GQA_KV_HEADS, HEAD_DIM))
        ctx_out[4].append(c_new)
        ctx_out[5].append(n_new.reshape(BATCH, 2, ML_HEADS, ML_DK))
        ctx_out[6].append(m_new.reshape(BATCH, 2, ML_HEADS))

    y_prompt = x[:N_CTX_TOK].reshape(BATCH, SEQ, D_MODEL)
    y_sample = x[N_CTX_TOK:].reshape(DEC_BATCH, DEC_SEQ, D_MODEL)
    return (y_prompt, y_sample) + tuple(jnp.stack(o, axis=1) for o in ctx_out)
```

```python
import functools

import numpy as np
import jax
import jax.numpy as jnp
from jax import lax
from jax.experimental import pallas as pl
from jax.experimental.pallas import tpu as pltpu

D_MODEL = 1024
BATCH = 16
SEQ = 256
DEPTH = 2
DEC_BATCH = 4
DEC_SEQ = 1024
PAST_LEN = 256
GRID_W = 64
GRID_ROWS = DEC_SEQ // GRID_W
HEAD_DIM = 64
NA_HEADS = 4
NA_WIN_ROWS = 8
NA_WIN_COLS = 16
GQA_Q_HEADS = 8
GQA_KV_HEADS = 2
GQA_GROUP = GQA_Q_HEADS // GQA_KV_HEADS
ML_HEADS = 4
ML_DK = 64
ML_DV = 64
ML_CHUNK = 64
N_BRANCH = 3
N_EXPERTS = 32
TOP_K = 4
D_FF_EXPERT = 1024
SWIGLU_LIMIT = 7.0
SWIGLU_ALPHA = 1.702
ROPE_THETA = 10000.0
RMS_EPS = 1e-6
NEG_INF = -1e30

NA_W = NA_HEADS * HEAD_DIM
GQ_W = GQA_Q_HEADS * HEAD_DIM
GKV_W = GQA_KV_HEADS * HEAD_DIM
ML_W = ML_HEADS * ML_DK
ML_GATES = 4 * ML_HEADS
NA_QKV_W = 3 * NA_W
GQA_QKV_W = GQ_W + 2 * GKV_W
ML_QKVO_W = 4 * ML_W
PROJ_W = NA_QKV_W + GQA_QKV_W + ML_QKVO_W
BG_W = N_BRANCH * D_MODEL

N_CTX_TOK = BATCH * SEQ
N_LAT_TOK = DEC_BATCH * DEC_SEQ
N_TOK = N_CTX_TOK + N_LAT_TOK
N_COND = 1 + DEC_BATCH
COND_ROWS = 8
LANE = 128
TOK_TILE = 256
N_TOK_TILES = N_TOK // TOK_TILE
N_CTX_TILES = N_CTX_TOK // TOK_TILE
LAT_TILES_PER_SEQ = DEC_SEQ // TOK_TILE
MOE_TILE = 512
N_SLOTS = N_TOK * TOP_K
MOE_ROWS = N_SLOTS + N_EXPERTS * MOE_TILE
N_MOE_TILES = MOE_ROWS // MOE_TILE
VMEM_LIMIT = 56 * 1024 * 1024

F32 = jnp.float32
BF16 = jnp.bfloat16
HIGHEST = lax.Precision.HIGHEST


def _mod_row(i):
    return jnp.where(i < N_CTX_TILES, 0, 1 + (i - N_CTX_TILES) // LAT_TILES_PER_SEQ)


def _rms(x):
    return x * lax.rsqrt(jnp.mean(x * x, axis=-1, keepdims=True) + RMS_EPS)


def _dot(a, b):
    return jnp.dot(a, b, preferred_element_type=F32)


def _dot_nt(a, b):
    return lax.dot_general(a, b, (((1,), (1,)), ((), ())), preferred_element_type=F32)


def _dot_tn(a, b):
    return lax.dot_general(a, b, (((0,), (0,)), ((), ())), preferred_element_type=F32)


def _params(*sem):
    return pltpu.CompilerParams(dimension_semantics=sem, vmem_limit_bytes=VMEM_LIMIT)


ADALN_TN = 512
ADALN_KC = 64


def _adaln_kernel(condt_ref, w_ref, b_ref, o_ref):
    def body(k, accs):
        k0 = pl.multiple_of(k * ADALN_KC, ADALN_KC)
        wc = w_ref[0, pl.ds(k0, ADALN_KC), :]
        sc = condt_ref[pl.ds(k0, ADALN_KC), :]
        sc = sc * jax.nn.sigmoid(sc)
        out = []
        for r in range(N_COND):
            prod = wc * sc[:, r:r + 1]
            out.append(accs[r] + jnp.sum(prod.reshape(ADALN_KC // 8, 8, ADALN_TN), axis=0))
        return tuple(out)

    accs = lax.fori_loop(0, D_MODEL // ADALN_KC, body,
                         tuple(jnp.zeros((8, ADALN_TN), F32) for _ in range(N_COND)))
    rows = [jnp.sum(a, axis=0, keepdims=True) for a in accs]
    rows += [jnp.zeros((1, ADALN_TN), F32)] * (COND_ROWS - N_COND)
    o_ref[0] = jnp.concatenate(rows, axis=0) + b_ref[0]


def _adaln(condt, w_mod, b_mod):
    n = 6 * D_MODEL
    return pl.pallas_call(
        _adaln_kernel,
        out_shape=jax.ShapeDtypeStruct((DEPTH, COND_ROWS, n), F32),
        grid=(DEPTH, n // ADALN_TN),
        in_specs=[pl.BlockSpec((D_MODEL, COND_ROWS), lambda l, j: (0, 0)),
                  pl.BlockSpec((1, D_MODEL, ADALN_TN), lambda l, j: (l, 0, j)),
                  pl.BlockSpec((1, 1, ADALN_TN), lambda l, j: (l, 0, j))],
        out_specs=pl.BlockSpec((1, COND_ROWS, ADALN_TN), lambda l, j: (l, 0, j)),
        compiler_params=_params("arbitrary", "arbitrary"),
        name="adaln",
    )(condt, w_mod, b_mod.reshape(DEPTH, 1, n))


PREP_ROWS = 128
IN_COLS = PROJ_W + ML_GATES + BG_W


def _split_bf16(x):
    hi = x.astype(BF16)
    return hi, (x - hi.astype(F32)).astype(BF16)


def _prep_kernel(w_ref, wa_ref, wglo_ref, wbg_ref):
    w = w_ref[...]
    wg_hi, wg_lo = _split_bf16(w[:, PROJ_W:PROJ_W + LANE])
    wa_ref[:, :PROJ_W] = w[:, :PROJ_W].astype(BF16)
    wa_ref[:, PROJ_W:] = wg_hi
    wglo_ref[...] = wg_lo
    wbg_ref[...] = w[:, PROJ_W + ML_GATES:].astype(BF16)


def _prep_w_in(w_in):
    blk = lambda w: pl.BlockSpec((None, PREP_ROWS, w), lambda l, r: (l, r, 0))
    return pl.pallas_call(
        _prep_kernel,
        out_shape=(jax.ShapeDtypeStruct((DEPTH, D_MODEL, PROJ_W + LANE), BF16),
                   jax.ShapeDtypeStruct((DEPTH, D_MODEL, LANE), BF16),
                   jax.ShapeDtypeStruct((DEPTH, D_MODEL, BG_W), BF16)),
        grid=(DEPTH, D_MODEL // PREP_ROWS),
        in_specs=[blk(IN_COLS)],
        out_specs=(blk(PROJ_W + LANE), blk(LANE), blk(BG_W)),
        compiler_params=_params("arbitrary", "arbitrary"),
        name="prep_w_in",
    )(w_in)


def _inproj_kernel(x_ref, mod_ref, ng_ref, wa_ref, wglo_ref, gb_ref,
                   na_ref, gqa_ref, ml_ref, g_ref, gt_ref):
    x = x_ref[...]
    h = _rms(x) * ng_ref[0:1, :]
    h = h * (1.0 + mod_ref[1:2, :]) + mod_ref[0:1, :]
    h_hi, h_lo = _split_bf16(h)
    y = _dot(h_hi, wa_ref[...])
    na_ref[...] = y[:, :NA_QKV_W]
    gqa_ref[...] = y[:, NA_QKV_W:NA_QKV_W + GQA_QKV_W]
    ml_ref[...] = y[:, NA_QKV_W + GQA_QKV_W:PROJ_W]
    g = y[:, PROJ_W:] + _dot(h_lo, wa_ref[:, PROJ_W:]) + _dot(h_hi, wglo_ref[...]) + gb_ref[...]
    g_ref[...] = g
    gt_ref[...] = g.T[:ML_GATES, :]


def _inproj(x, mod, norm_g, wa, wglo, gb):
    tile = lambda w: pl.BlockSpec((TOK_TILE, w), lambda i: (i, 0))
    full = lambda a: pl.BlockSpec(a.shape, lambda i: (0,) * a.ndim)
    return pl.pallas_call(
        _inproj_kernel,
        out_shape=(jax.ShapeDtypeStruct((N_TOK, NA_QKV_W), F32),
                   jax.ShapeDtypeStruct((N_TOK, GQA_QKV_W), F32),
                   jax.ShapeDtypeStruct((N_TOK, ML_QKVO_W), F32),
                   jax.ShapeDtypeStruct((N_TOK, LANE), F32),
                   jax.ShapeDtypeStruct((ML_GATES, N_TOK), F32)),
        grid=(N_TOK_TILES,),
        in_specs=[tile(D_MODEL),
                  pl.BlockSpec((None, 8, D_MODEL), lambda i: (_mod_row(i), 0, 0)),
                  full(norm_g), full(wa), full(wglo), full(gb)],
        out_specs=(tile(NA_QKV_W), tile(GQA_QKV_W), tile(ML_QKVO_W), tile(LANE),
                   pl.BlockSpec((ML_GATES, TOK_TILE), lambda i: (0, i))),
        compiler_params=_params("arbitrary"),
        name="inproj",
    )(x, mod, norm_g, wa, wglo, gb)


def _head(x, h):
    return x[:, h * HEAD_DIM:(h + 1) * HEAD_DIM]


def _head_rms(x, n_heads, g):
    return jnp.concatenate([_rms(_head(x, h)) * g for h in range(n_heads)], axis=-1)


def _softmax_pv(score_blocks, value_blocks):
    m = score_blocks[0].max(axis=-1, keepdims=True)
    for s in score_blocks[1:]:
        m = jnp.maximum(m, s.max(axis=-1, keepdims=True))
    ps = [jnp.exp(s - m) for s in score_blocks]
    l = ps[0].sum(axis=-1, keepdims=True)
    for p in ps[1:]:
        l = l + p.sum(axis=-1, keepdims=True)
    inv = 1.0 / l
    o = None
    for p, v in zip(ps, value_blocks):
        t = _dot((p * inv).astype(BF16), v.astype(BF16))
        o = t if o is None else o + t
    return o


_ATT_SCALE = HEAD_DIM ** -0.5


def _ctx_attn_kernel(na_ref, gqa_ref, qkn_ref, o_ref, kn_ref):
    na = na_ref[...]
    outs = []
    for h in range(NA_HEADS):
        q = _head(na, h) * _ATT_SCALE
        k = _head(na, NA_HEADS + h)
        v = _head(na, 2 * NA_HEADS + h)
        s = _dot_nt(q.astype(BF16), k.astype(BF16))
        outs.append(_softmax_pv([s], [v]))
    gq = gqa_ref[:, :GQ_W]
    gk = gqa_ref[:, GQ_W:GQ_W + GKV_W]
    gv = gqa_ref[:, GQ_W + GKV_W:]
    qn = _head_rms(gq, GQA_Q_HEADS, qkn_ref[0:1, :])
    kn = _head_rms(gk, GQA_KV_HEADS, qkn_ref[1:2, :])
    kn_ref[...] = kn
    for h in range(GQA_Q_HEADS):
        kv = h // GQA_GROUP
        s = _dot_nt((_head(qn, h) * _ATT_SCALE).astype(BF16), _head(kn, kv).astype(BF16))
        outs.append(_softmax_pv([s], [_head(gv, kv)]))
    o_ref[...] = jnp.concatenate(outs, axis=-1)


def _ctx_attn(na, gqa, qk_norm):
    return pl.pallas_call(
        _ctx_attn_kernel,
        out_shape=(jax.ShapeDtypeStruct((N_CTX_TOK, NA_W + GQ_W), F32),
                   jax.ShapeDtypeStruct((N_CTX_TOK, GKV_W), F32)),
        grid=(BATCH,),
        in_specs=[pl.BlockSpec((SEQ, NA_QKV_W), lambda b: (b, 0)),
                  pl.BlockSpec((SEQ, GQA_QKV_W), lambda b: (b, 0)),
                  pl.BlockSpec((2, HEAD_DIM), lambda b: (0, 0))],
        out_specs=(pl.BlockSpec((SEQ, NA_W + GQ_W), lambda b: (b, 0)),
                   pl.BlockSpec((SEQ, GKV_W), lambda b: (b, 0))),
        compiler_params=_params("arbitrary"),
        name="ctx_attn",
    )(na, gqa, qk_norm)


def _lat_na_kernel(q_ref, kv_ref, ck_ref, cv_ref, tab_ref, o_ref):
    r = pl.program_id(1)
    row_start = jnp.clip(r - NA_WIN_ROWS // 2, 0, GRID_ROWS - NA_WIN_ROWS)
    k0 = pl.multiple_of(row_start * GRID_W, GRID_W)
    n_loc = NA_WIN_ROWS * GRID_W
    kwin = kv_ref[pl.ds(k0, n_loc), NA_W:2 * NA_W]
    vwin = kv_ref[pl.ds(k0, n_loc), 2 * NA_W:3 * NA_W]
    q = q_ref[:, :NA_W]
    ck = ck_ref[...]
    cv = cv_ref[...]
    outs = []
    for h in range(NA_HEADS):
        qh = (_head(q, h) * _ATT_SCALE).astype(BF16)
        s_loc = _dot_nt(qh, _head(kwin, h).astype(BF16)) + tab_ref[h, 0]
        s_ctx = _dot_nt(qh, _head(ck, h).astype(BF16))
        outs.append(_softmax_pv([s_loc, s_ctx], [_head(vwin, h), _head(cv, h)]))
    o_ref[...] = jnp.concatenate(outs, axis=-1)


def _na_variant(r):
    return r - jnp.clip(r - NA_WIN_ROWS // 2, 0, GRID_ROWS - NA_WIN_ROWS)


def _lat_na(na, ck, cv, tab):
    ctx_rows = N_CTX_TOK // GRID_W
    ctx_seqs = N_CTX_TOK // DEC_SEQ
    return pl.pallas_call(
        _lat_na_kernel,
        out_shape=jax.ShapeDtypeStruct((N_LAT_TOK, NA_W), F32),
        grid=(DEC_BATCH, GRID_ROWS),
        in_specs=[pl.BlockSpec((GRID_W, NA_QKV_W), lambda b, r: (ctx_rows + b * GRID_ROWS + r, 0)),
                  pl.BlockSpec((DEC_SEQ, NA_QKV_W), lambda b, r: (ctx_seqs + b, 0)),
                  pl.BlockSpec((None, PAST_LEN, NA_W), lambda b, r: (b, 0, 0)),
                  pl.BlockSpec((None, PAST_LEN, NA_W), lambda b, r: (b, 0, 0)),
                  pl.BlockSpec((NA_HEADS, 1, GRID_W, NA_WIN_ROWS * GRID_W),
                               lambda b, r: (0, _na_variant(r), 0, 0))],
        out_specs=pl.BlockSpec((GRID_W, NA_W), lambda b, r: (b * GRID_ROWS + r, 0)),
        compiler_params=_params("arbitrary", "arbitrary"),
        name="lat_na",
    )(na, na, ck, cv, tab)


LAT_GQA_TQ = 256


def _rope(x, cos, sin):
    w = x.shape[-1]
    lane = lax.broadcasted_iota(jnp.int32, x.shape, 1)
    first = (lane % 32) < 16
    partner = jnp.where(first, pltpu.roll(x, w - 16, 1), pltpu.roll(x, 16, 1))
    return x * cos + partner * sin


def _lat_gqa_kernel(q_ref, kv_ref, ck_ref, cv_ref, qkn_ref, cq_ref, sq_ref, ckk_ref, skk_ref, o_ref):
    gq = q_ref[:, :GQ_W]
    gk = kv_ref[:, GQ_W:GQ_W + GKV_W]
    gv = kv_ref[:, GQ_W + GKV_W:]
    qn = _rope(_head_rms(gq, GQA_Q_HEADS, qkn_ref[0:1, :]), cq_ref[...], sq_ref[...])
    kn = _rope(_head_rms(gk, GQA_KV_HEADS, qkn_ref[1:2, :]), ckk_ref[...], skk_ref[...])
    ck = ck_ref[...]
    cv = cv_ref[...]
    outs = []
    for h in range(GQA_Q_HEADS):
        kv = h // GQA_GROUP
        qh = (_head(qn, h) * _ATT_SCALE).astype(BF16)
        s_loc = _dot_nt(qh, _head(kn, kv).astype(BF16))
        s_ctx = _dot_nt(qh, _head(ck, kv).astype(BF16))
        outs.append(_softmax_pv([s_loc, s_ctx], [_head(gv, kv), _head(cv, kv)]))
    o_ref[...] = jnp.concatenate(outs, axis=-1)


def _lat_gqa(gqa, ck, cv, qk_norm, cos_q, sin_q, cos_k, sin_k):
    nq = DEC_SEQ // LAT_GQA_TQ
    ctx_qblocks = N_CTX_TOK // LAT_GQA_TQ
    ctx_seqs = N_CTX_TOK // DEC_SEQ
    full = lambda a: pl.BlockSpec(a.shape, lambda b, j: (0,) * a.ndim)
    return pl.pallas_call(
        _lat_gqa_kernel,
        out_shape=jax.ShapeDtypeStruct((N_LAT_TOK, GQ_W), F32),
        grid=(DEC_BATCH, nq),
        in_specs=[pl.BlockSpec((LAT_GQA_TQ, GQA_QKV_W), lambda b, j: (ctx_qblocks + b * nq + j, 0)),
                  pl.BlockSpec((DEC_SEQ, GQA_QKV_W), lambda b, j: (ctx_seqs + b, 0)),
                  pl.BlockSpec((None, PAST_LEN, GKV_W), lambda b, j: (b, 0, 0)),
                  pl.BlockSpec((None, PAST_LEN, GKV_W), lambda b, j: (b, 0, 0)),
                  full(qk_norm),
                  pl.BlockSpec((LAT_GQA_TQ, GQ_W), lambda b, j: (j, 0)),
                  pl.BlockSpec((LAT_GQA_TQ, GQ_W), lambda b, j: (j, 0)),
                  full(cos_k), full(sin_k)],
        out_specs=pl.BlockSpec((LAT_GQA_TQ, GQ_W), lambda b, j: (b * nq + j, 0)),
        compiler_params=_params("arbitrary", "arbitrary"),
        name="lat_gqa",
    )(gqa, gqa, ck, cv, qk_norm, cos_q, sin_q, cos_k, sin_k)


def _mlstm_kernel(q_ref, k_ref, v_ref, o_ref, g_ref, gt_ref, mln_ref, c0_ref, n0r_ref, n0c_ref, m0_ref,
                  out_ref, c_out, n_out, m_out):
    S = q_ref.shape[0]
    L, H, W = ML_CHUNK, ML_HEADS, ML_W
    nc = S // L
    q3 = q_ref[...].reshape(nc, L, W)
    k3 = (k_ref[...] * (ML_DK ** -0.5)).reshape(nc, L, W)
    vb = v_ref[...].reshape(nc, L, W).astype(BF16)
    lane3 = lax.broadcasted_iota(jnp.int32, (1, 1, W), 2)
    hm = [lane3 // ML_DK == h for h in range(H)]
    expand = lambda col: sum(jnp.where(hm[h], col[:, h], 0.0) for h in range(H))
    qst = jnp.concatenate([jnp.where(hm[h], q3, 0.0) for h in range(H)], axis=1).astype(BF16)
    s_raw = jnp.einsum('cqd,ckd->cqk', qst, k3.astype(BF16),
                       preferred_element_type=F32).reshape(nc, H, L, L)
    g3 = g_ref[...].reshape(nc, L, LANE)
    lg3 = jax.nn.log_sigmoid(g3)
    gt = gt_ref[...]
    lgt = jax.nn.log_sigmoid(gt)
    si = lax.broadcasted_iota(jnp.int32, (L, L), 0)
    ui = lax.broadcasted_iota(jnp.int32, (L, L), 1)
    ri = lax.broadcasted_iota(jnp.int32, (W, 2 * W), 0)
    ci = lax.broadcasted_iota(jnp.int32, (W, 2 * W), 1)
    blockdiag = (ri // ML_DK == (ci % W) // ML_DV).astype(F32)
    v1 = jnp.concatenate([vb, jnp.ones((nc, L, W), BF16)], axis=2)
    h_sum = [None] * nc
    for d in range(2):
        backward = d == 1
        gi = 2 * d * H
        gf = gi + H
        col = lambda a, c0: jnp.stack([a[:, :, c0 + h:c0 + h + 1] for h in range(H)], axis=1)
        row = lambda a, r0: jnp.stack([jnp.stack([a[r0 + h:r0 + h + 1, j * L:(j + 1) * L]
                                                  for h in range(H)], axis=0) for j in range(nc)], axis=0)
        i_col, lf_col = col(g3, gi), col(lg3, gf)
        i_row, lf_row = row(gt, gi), row(lgt, gf)
        mask = (ui >= si) if backward else (ui <= si)
        mask_t = (si >= ui) if backward else (si <= ui)
        bcum_col = jnp.sum(jnp.where(mask, lf_row, 0.0), axis=3, keepdims=True)
        bcum_row = jnp.sum(jnp.where(mask_t, lf_col, 0.0), axis=2, keepdims=True)
        b_last = jnp.sum(lf_col, axis=2, keepdims=True)
        log_d = jnp.where(mask, bcum_col - bcum_row + i_row, NEG_INF)
        m_loc = jnp.max(log_d, axis=3, keepdims=True)
        s = s_raw * jnp.exp(log_d - m_loc)
        den_loc = jnp.sum(s, axis=3, keepdims=True)
        num_st = jnp.einsum('cqk,ckd->cqd', s.reshape(nc, H * L, L).astype(BF16), vb,
                            preferred_element_type=F32).reshape(nc, H, L, W)
        num_loc = sum(jnp.where(hm[h], num_st[:, h], 0.0) for h in range(H))
        bcum_x, mloc_x, den_x = expand(bcum_col), expand(m_loc), expand(den_loc)
        log_w = b_last - bcum_col + i_col
        m_w = jnp.max(log_w, axis=2, keepdims=True)
        kw = k3 * expand(jnp.exp(log_w - m_w))
        dn_row = jnp.sum(kw, axis=1, keepdims=True)
        kwb = kw.astype(BF16)
        blast_x, mw_x = expand(b_last), expand(m_w)

        c_bd = jnp.concatenate(
            [jnp.concatenate([c0_ref[0, d, h] if hh == h else jnp.zeros((ML_DK, ML_DV), F32)
                              for hh in range(H)], axis=1) for h in range(H)], axis=0)
        cn = jnp.concatenate([c_bd, jnp.broadcast_to(n0c_ref[0, d], (W, W))], axis=1) * blockdiag
        n_row = n0r_ref[0, d]
        m = m0_ref[0, d]
        for jj in range(nc):
            j = nc - 1 - jj if backward else jj
            m_inter = bcum_x[j] + m
            m_s = jnp.maximum(m_inter, mloc_x[j])
            a_loc = jnp.exp(mloc_x[j] - m_s)
            a_int = jnp.exp(m_inter - m_s)
            qcn = _dot(q3[j].astype(BF16), cn.astype(BF16))
            num = a_loc * num_loc[j] + a_int * qcn[:, :W]
            den = a_loc * den_x[j] + a_int * qcn[:, W:]
            h_out = num / jnp.maximum(jnp.abs(den), jnp.exp(-m_s))
            h_sum[j] = h_out if d == 0 else h_sum[j] + h_out
            m_new = jnp.maximum(blast_x[j] + m, mw_x[j])
            decay = jnp.exp(blast_x[j] + m - m_new)
            scale = jnp.exp(mw_x[j] - m_new)
            dcn = _dot_tn(kwb[j], v1[j]) * blockdiag
            cn = cn * jnp.concatenate([decay, decay], axis=1) + dcn * jnp.concatenate([scale, scale], axis=1)
            n_row = decay * n_row + scale * dn_row[j]
            m = m_new
        for h in range(H):
            c_out[0, d, h] = cn[h * ML_DK:(h + 1) * ML_DK, h * ML_DV:(h + 1) * ML_DV]
        n_out[0, d] = n_row
        m_out[0, d] = m
    hs = jnp.concatenate(h_sum, axis=0)
    hn = jnp.concatenate([_rms(_head(hs, hd)) for hd in range(H)], axis=-1) * mln_ref[...]
    out_ref[...] = hn * jax.nn.sigmoid(o_ref[...])


def _mlstm(ml, g, gt, ml_norm, c0, n0, m0, n_seq, seq_len, tok0):
    s0 = tok0 // seq_len
    n0_row = n0.reshape(n_seq, 2, 1, ML_W)
    n0_col = n0.reshape(n_seq, 2, ML_W, 1)
    m0_x = jnp.repeat(m0, ML_DK, axis=-1).reshape(n_seq, 2, 1, ML_W)
    colblk = lambda c: pl.BlockSpec((seq_len, ML_W), lambda s: (s0 + s, c))
    state = lambda a: pl.BlockSpec((1,) + a.shape[1:], lambda s: (s,) + (0,) * (a.ndim - 1))
    out, c_new, n_new, m_new = pl.pallas_call(
        _mlstm_kernel,
        out_shape=(jax.ShapeDtypeStruct((n_seq * seq_len, ML_W), F32),
                   jax.ShapeDtypeStruct(c0.shape, F32),
                   jax.ShapeDtypeStruct(n0_row.shape, F32),
                   jax.ShapeDtypeStruct(m0_x.shape, F32)),
        grid=(n_seq,),
        in_specs=[colblk(0), colblk(1), colblk(2), colblk(3),
                  pl.BlockSpec((seq_len, LANE), lambda s: (s0 + s, 0)),
                  pl.BlockSpec((ML_GATES, seq_len), lambda s: (0, s0 + s)),
                  pl.BlockSpec(ml_norm.shape, lambda s: (0, 0)),
                  state(c0), state(n0_row), state(n0_col), state(m0_x)],
        out_specs=(pl.BlockSpec((seq_len, ML_W), lambda s: (s, 0)),
                   state(c0), state(n0_row), state(m0_x)),
        compiler_params=_params("arbitrary"),
        name="mlstm",
    )(ml, ml, ml, ml, g, gt, ml_norm, c0, n0_row, n0_col, m0_x)
    return (out, c_new, n_new.reshape(n_seq, 2, ML_HEADS, ML_DK),
            m_new.reshape(n_seq, 2, ML_HEADS, ML_DK)[..., 0])


def _merge_kernel(x_ref, mod_ref, ng_ref, o_ref, oml_ref,
                  wbg_ref, wna_ref, wgqa_ref, wml_ref, wout_ref, rwhi_ref, rwlo_ref, rb_ref,
                  x1_ref, h2_ref, route_ref, tw_ref, cnt_ref):
    i = pl.program_id(0)
    x = x_ref[...]
    h = _rms(x) * ng_ref[0:1, :]
    h = (h * (1.0 + mod_ref[1:2, :]) + mod_ref[0:1, :]).astype(BF16)
    branches = ((o_ref[:, :NA_W], wna_ref), (o_ref[:, NA_W:], wgqa_ref), (oml_ref[...], wml_ref))
    z = None
    for b, (o_b, w_b) in enumerate(branches):
        gate = jax.nn.sigmoid(_dot(h, wbg_ref[:, b * D_MODEL:(b + 1) * D_MODEL]))
        t = gate * _dot(o_b.astype(BF16), w_b[...])
        z = t if z is None else z + t
    mix = _dot(z.astype(BF16), wout_ref[...])
    x1 = x + mod_ref[2:3, :] * (_rms(mix) * ng_ref[1:2, :])
    x1_ref[...] = x1
    h2 = _rms(x1) * ng_ref[2:3, :]
    h2 = h2 * (1.0 + mod_ref[4:5, :]) + mod_ref[3:4, :]
    half = D_MODEL // 2
    hi = lax.bitcast_convert_type(h2[:, :half].astype(BF16).astype(F32), jnp.int32)
    lo = lax.bitcast_convert_type(h2[:, half:].astype(BF16).astype(F32), jnp.int32)
    h2_ref[...] = hi | lax.shift_right_logical(lo, 16)
    h2_hi, h2_lo = _split_bf16(h2)
    logits = (_dot(h2_hi, rwhi_ref[...]) + _dot(h2_lo, rwhi_ref[...]) + _dot(h2_hi, rwlo_ref[...])
              + rb_ref[...])
    lane = lax.broadcasted_iota(jnp.int32, logits.shape, 1)
    lane_f = lane.astype(F32)
    val_out = jnp.zeros(logits.shape, F32)
    onehot = jnp.zeros(logits.shape, F32)
    sels = []
    v0 = None
    denom = None
    for k in range(TOP_K):
        mx = jnp.max(logits, axis=-1, keepdims=True)
        sel = jnp.min(jnp.where(logits == mx, lane_f, float(LANE)), axis=-1, keepdims=True).astype(jnp.int32)
        if k == 0:
            v0 = mx
        e = jnp.exp(mx - v0)
        denom = e if denom is None else denom + e
        val_out = jnp.where(lane == k, e, val_out)
        onehot = jnp.where(lane == sel, 1.0, onehot)
        logits = jnp.where(lane == sel, NEG_INF, logits)
        sels.append(sel)
    tw_ref[...] = val_out / denom

    @pl.when(i == 0)
    def _():
        cnt_ref[...] = jnp.zeros(cnt_ref.shape, F32)

    ti = lax.broadcasted_iota(jnp.int32, (TOK_TILE, TOK_TILE), 0)
    tj = lax.broadcasted_iota(jnp.int32, (TOK_TILE, TOK_TILE), 1)
    earlier = _dot((tj < ti).astype(BF16), onehot.astype(BF16)) + cnt_ref[...]
    route = jnp.zeros(logits.shape, jnp.int32)
    for k, sel in enumerate(sels):
        rank = jnp.sum(jnp.where(lane == sel, earlier, 0.0), axis=-1, keepdims=True).astype(jnp.int32)
        route = jnp.where(lane == k, sel, route)
        route = jnp.where(lane == TOP_K + k, rank, route)
    route_ref[...] = route
    cnt_ref[...] = cnt_ref[...] + jnp.sum(onehot, axis=0, keepdims=True)


def _merge(x, mod, norm_g, o_attn, o_ml, wbg, wna, wgqa, wml, wout, rw_hi, rw_lo, rb):
    tile = lambda w: pl.BlockSpec((TOK_TILE, w), lambda i: (i, 0))
    full = lambda a: pl.BlockSpec(a.shape, lambda i: (0,) * a.ndim)
    return pl.pallas_call(
        _merge_kernel,
        out_shape=(jax.ShapeDtypeStruct((N_TOK, D_MODEL), F32),
                   jax.ShapeDtypeStruct((N_TOK, D_MODEL // 2), jnp.int32),
                   jax.ShapeDtypeStruct((N_TOK, LANE), jnp.int32),
                   jax.ShapeDtypeStruct((N_TOK, LANE), F32),
                   jax.ShapeDtypeStruct((1, LANE), F32)),
        grid=(N_TOK_TILES,),
        in_specs=[tile(D_MODEL),
                  pl.BlockSpec((None, 8, D_MODEL), lambda i: (_mod_row(i), 0, 0)),
                  full(norm_g), tile(NA_W + GQ_W), tile(ML_W),
                  full(wbg), full(wna), full(wgqa), full(wml), full(wout),
                  full(rw_hi), full(rw_lo), full(rb)],
        out_specs=(tile(D_MODEL), tile(D_MODEL // 2), tile(LANE), tile(LANE),
                   pl.BlockSpec((1, LANE), lambda i: (0, 0))),
        compiler_params=_params("arbitrary"),
        name="merge",
    )(x, mod, norm_g, o_attn, o_ml, wbg, wna, wgqa, wml, wout, rw_hi, rw_lo, rb)


MOE_CAST_ROWS = 128


def _moe_kernel(te_ref, nv_ref, x_ref, wgu_ref, bgu_ref, wd_ref, bd_ref, o_ref, wgu_s, wd_s):
    i = pl.program_id(0)
    prev = te_ref[jnp.maximum(i - 1, 0)]
    new_expert = jnp.logical_or(i == 0, te_ref[i] != prev)

    @pl.when(new_expert)
    def _():
        def cast(r, carry):
            r0 = pl.multiple_of(r * MOE_CAST_ROWS, MOE_CAST_ROWS)
            wgu_s[pl.ds(r0, MOE_CAST_ROWS), :] = wgu_ref[pl.ds(r0, MOE_CAST_ROWS), :].astype(BF16)
            wd_s[pl.ds(r0, MOE_CAST_ROWS), :] = wd_ref[pl.ds(r0, MOE_CAST_ROWS), :].astype(BF16)
            return carry
        lax.fori_loop(0, D_MODEL // MOE_CAST_ROWS, cast, 0)

    @pl.when(i < nv_ref[0])
    def _():
        half = D_MODEL // 2
        xp = x_ref[...]
        x_hi = lax.bitcast_convert_type(xp & jnp.int32(-65536), F32).astype(BF16)
        x_lo = lax.bitcast_convert_type(lax.shift_left(xp, 16), F32).astype(BF16)
        gu = _dot(x_hi, wgu_s[:half, :]) + _dot(x_lo, wgu_s[half:, :]) + bgu_ref[...]
        gate = jnp.minimum(gu[:, :D_FF_EXPERT], SWIGLU_LIMIT)
        up = jnp.clip(gu[:, D_FF_EXPERT:], -SWIGLU_LIMIT, SWIGLU_LIMIT)
        act = (up + 1.0) * (gate * jax.nn.sigmoid(SWIGLU_ALPHA * gate))
        o_ref[...] = _dot(act.astype(BF16), wd_s[...]) + bd_ref[...]

    @pl.when(i >= nv_ref[0])
    def _():
        o_ref[...] = jnp.zeros(o_ref.shape, F32)


def _moe(layer, tile_expert, n_valid, xs, wgu, bgu, wd, bd):
    assert D_FF_EXPERT == D_MODEL
    expert = lambda r, c: pl.BlockSpec((None, None, r, c), lambda i, te, nv: (layer, te[i], 0, 0))
    grid_spec = pltpu.PrefetchScalarGridSpec(
        num_scalar_prefetch=2,
        grid=(N_MOE_TILES,),
        in_specs=[pl.BlockSpec((MOE_TILE, D_MODEL // 2), lambda i, te, nv: (i, 0)),
                  expert(D_MODEL, 2 * D_FF_EXPERT), expert(1, 2 * D_FF_EXPERT),
                  expert(D_FF_EXPERT, D_MODEL), expert(1, D_MODEL)],
        out_specs=pl.BlockSpec((MOE_TILE, D_MODEL), lambda i, te, nv: (i, 0)),
        scratch_shapes=[pltpu.VMEM((D_MODEL, 2 * D_FF_EXPERT), BF16),
                        pltpu.VMEM((D_FF_EXPERT, D_MODEL), BF16)],
    )
    return pl.pallas_call(
        _moe_kernel,
        out_shape=jax.ShapeDtypeStruct((MOE_ROWS, D_MODEL), F32),
        grid_spec=grid_spec,
        compiler_params=_params("arbitrary"),
        name="moe",
    )(tile_expert, n_valid, xs, wgu, bgu.reshape(DEPTH, N_EXPERTS, 1, -1), wd,
      bd.reshape(DEPTH, N_EXPERTS, 1, -1))


def _route(route, counts):
    top_idx = route[:, :TOP_K]
    rank = route[:, TOP_K:2 * TOP_K]
    counts = counts[0, :N_EXPERTS].astype(jnp.int32)
    padded = ((counts + MOE_TILE - 1) // MOE_TILE) * MOE_TILE
    ends = jnp.cumsum(padded)
    starts = ends - padded
    experts = jnp.arange(N_EXPERTS, dtype=jnp.int32)
    start_of = jnp.sum(jnp.where(top_idx[:, :, None] == experts, starts, 0), axis=-1)
    slot_row = (start_of + rank).T.reshape(N_SLOTS)
    slot_token = jnp.tile(jnp.arange(N_TOK, dtype=jnp.int32), TOP_K)
    row_token = jnp.zeros((MOE_ROWS,), jnp.int32).at[slot_row].set(slot_token)
    n_valid = (ends[-1] // MOE_TILE).astype(jnp.int32)
    tile_start = jnp.arange(N_MOE_TILES, dtype=jnp.int32) * MOE_TILE
    tile_expert = jnp.sum((tile_start[:, None] >= ends[None, :]).astype(jnp.int32), axis=1)
    last_expert = jnp.sum(jnp.where(jnp.arange(N_MOE_TILES) == n_valid - 1, tile_expert, 0))
    tile_expert = jnp.where(jnp.arange(N_MOE_TILES) < n_valid, tile_expert, last_expert)
    return row_token, slot_row, tile_expert.astype(jnp.int32), n_valid.reshape(1)


def _final_kernel(x1_ref, y_ref, tw_ref, mod_ref, ng_ref, o_ref):
    tw = tw_ref[...]
    moe = sum(tw[:, k:k + 1] * y_ref[k] for k in range(TOP_K))
    o_ref[...] = x1_ref[...] + mod_ref[5:6, :] * (_rms(moe) * ng_ref[3:4, :])


def _final(x1, y_slots, top_w, mod, norm_g):
    tile = lambda w: pl.BlockSpec((TOK_TILE, w), lambda i: (i, 0))
    return pl.pallas_call(
        _final_kernel,
        out_shape=jax.ShapeDtypeStruct((N_TOK, D_MODEL), F32),
        grid=(N_TOK_TILES,),
        in_specs=[tile(D_MODEL),
                  pl.BlockSpec((TOP_K, TOK_TILE, D_MODEL), lambda i: (0, i, 0)),
                  tile(LANE),
                  pl.BlockSpec((None, 8, D_MODEL), lambda i: (_mod_row(i), 0, 0)),
                  pl.BlockSpec(norm_g.shape, lambda i: (0, 0))],
        out_specs=tile(D_MODEL),
        compiler_params=_params("arbitrary"),
        name="final",
    )(x1, y_slots, top_w, mod, norm_g)


def _rope_tables():
    nf = HEAD_DIM // 4
    t = np.arange(DEC_SEQ)
    pos = np.stack([t // GRID_W, t % GRID_W], axis=-1).astype(np.float32)
    inv_freq = (np.float32(ROPE_THETA) ** (-np.arange(nf, dtype=np.float32) / nf)).astype(np.float32)
    ang = pos[:, :, None] * inv_freq
    cos = np.cos(ang.astype(np.float64))
    sin = np.sin(ang.astype(np.float64))
    cos_h = np.concatenate([cos[:, 0], cos[:, 0], cos[:, 1], cos[:, 1]], axis=-1)
    sin_h = np.concatenate([-sin[:, 0], sin[:, 0], -sin[:, 1], sin[:, 1]], axis=-1)
    tile = lambda a, n: jnp.asarray(np.tile(a, (1, n)), F32)
    return (tile(cos_h, GQA_Q_HEADS), tile(sin_h, GQA_Q_HEADS),
            tile(cos_h, GQA_KV_HEADS), tile(sin_h, GQA_KV_HEADS))


def _na_bias_table(rel_bias):
    qc = np.arange(GRID_W)[:, None]
    kc = np.arange(GRID_W)[None, :]
    dc = np.clip(kc - qc + NA_WIN_COLS - 1, 0, 2 * NA_WIN_COLS - 2)
    cs = np.clip(qc - NA_WIN_COLS // 2, 0, GRID_W - NA_WIN_COLS)
    valid = (kc >= cs) & (kc < cs + NA_WIN_COLS)
    pick = (dc.reshape(-1)[None, :] == np.arange(2 * NA_WIN_COLS - 1)[:, None]).astype(np.float32)
    cols = jnp.einsum('lhrc,cx->lhrx', rel_bias.astype(F32), jnp.asarray(pick), precision=HIGHEST)
    cols = cols.reshape(DEPTH, NA_HEADS, 2 * NA_WIN_ROWS - 1, GRID_W, GRID_W)
    cols = jnp.where(jnp.asarray(valid), cols, NEG_INF)
    variants = []
    for v in range(NA_WIN_ROWS):
        rows = cols[:, :, NA_WIN_ROWS - 1 - v:2 * NA_WIN_ROWS - 1 - v]
        variants.append(rows.transpose(0, 1, 3, 2, 4).reshape(DEPTH, NA_HEADS, GRID_W, NA_WIN_ROWS * GRID_W))
    return jnp.stack(variants, axis=2)


def kernel(x_prompt, x_sample, cache_na_k, cache_na_v, cache_gqa_k, cache_gqa_v, state_ml_c, state_ml_n,
           state_ml_m, c, c_ctx, w_mod, b_mod, norm_g, w_in, na_rel_bias, gqa_qk_norm, ml_gate_bias,
           ml_norm_g, w_branch_na, w_branch_gqa, w_branch_ml, w_out, router_w, router_b, w_gate_up,
           b_gate_up, w_down, b_down):
    x = jnp.concatenate([x_prompt.reshape(N_CTX_TOK, D_MODEL), x_sample.reshape(N_LAT_TOK, D_MODEL)], axis=0)
    cond = jnp.concatenate([c_ctx[None, :], c, jnp.zeros((COND_ROWS - N_COND, D_MODEL), F32)], axis=0)
    mod_all = _adaln(cond.T, w_mod, b_mod).reshape(DEPTH, COND_ROWS, 6, D_MODEL)
    mod_all = jnp.pad(mod_all, ((0, 0), (0, 0), (0, 2), (0, 0)))
    cos_q, sin_q, cos_k, sin_k = _rope_tables()
    wa_all, wglo_all, wbg_all = _prep_w_in(w_in)
    tab_all = _na_bias_table(na_rel_bias)
    wna_all, wgqa_all, wml_all, wout_all = (w.astype(BF16) for w in (w_branch_na, w_branch_gqa,
                                                                     w_branch_ml, w_out))
    rw_all = jnp.pad(router_w, ((0, 0), (0, 0), (0, LANE - N_EXPERTS)))
    rwhi_all = rw_all.astype(BF16)
    rwlo_all = (rw_all - rwhi_all.astype(F32)).astype(BF16)
    rb_all = jnp.pad(router_b.reshape(DEPTH, 1, N_EXPERTS), ((0, 0), (0, 0), (0, LANE - N_EXPERTS)),
                     constant_values=NEG_INF)
    gb_all = jnp.pad(ml_gate_bias.reshape(DEPTH, 1, ML_GATES), ((0, 0), (0, 0), (0, LANE - ML_GATES)))
    zero_state = (jnp.zeros((BATCH, 2, ML_HEADS, ML_DK, ML_DV), F32),
                  jnp.zeros((BATCH, 2, ML_HEADS, ML_DK), F32), jnp.zeros((BATCH, 2, ML_HEADS), F32))

    ctx_out = [[] for _ in range(7)]
    for l in range(DEPTH):
        mod = mod_all[l]
        na, gqa, ml, g, gt = _inproj(x, mod, norm_g[l], wa_all[l], wglo_all[l], gb_all[l])

        o_ctx, kn_ctx = _ctx_attn(na, gqa, gqa_qk_norm[l])
        o_na = _lat_na(na, cache_na_k[:, l].reshape(DEC_BATCH, PAST_LEN, NA_W),
                       cache_na_v[:, l].reshape(DEC_BATCH, PAST_LEN, NA_W), tab_all[l])
        o_gqa = _lat_gqa(gqa, cache_gqa_k[:, l].reshape(DEC_BATCH, PAST_LEN, GKV_W),
                         cache_gqa_v[:, l].reshape(DEC_BATCH, PAST_LEN, GKV_W), gqa_qk_norm[l],
                         cos_q, sin_q, cos_k, sin_k)
        o_attn = jnp.concatenate([o_ctx, jnp.concatenate([o_na, o_gqa], axis=1)], axis=0)
        ml_norm = ml_norm_g[l].reshape(1, ML_W)
        oml_c, c_new, n_new, m_new = _mlstm(ml, g, gt, ml_norm, *zero_state, BATCH, SEQ, 0)
        oml_l, _, _, _ = _mlstm(ml, g, gt, ml_norm, state_ml_c[:, l], state_ml_n[:, l], state_ml_m[:, l],
                                DEC_BATCH, DEC_SEQ, N_CTX_TOK)
        o_ml = jnp.concatenate([oml_c, oml_l], axis=0)

        x1, h2, route, top_w, counts = _merge(
            x, mod, norm_g[l], o_attn, o_ml, wbg_all[l], wna_all[l], wgqa_all[l], wml_all[l],
            wout_all[l], rwhi_all[l], rwlo_all[l], rb_all[l])

        row_token, slot_row, tile_expert, n_valid = _route(route, counts)
        xs = jnp.take(h2, row_token, axis=0)
        y = _moe(l, tile_expert, n_valid, xs, w_gate_up, b_gate_up, w_down, b_down)
        y_slots = jnp.take(y, slot_row, axis=0).reshape(TOP_K, N_TOK, D_MODEL)
        x = _final(x1, y_slots, top_w, mod, norm_g[l])

        ctx_out[0].append(na[:N_CTX_TOK, NA_W:2 * NA_W].reshape(BATCH, SEQ, NA_HEADS, HEAD_DIM))
        ctx_out[1].append(na[:N_CTX_TOK, 2 * NA_W:].reshape(BATCH, SEQ, NA_HEADS, HEAD_DIM))
        ctx_out[2].append(kn_ctx.reshape(BATCH, SEQ, GQA_KV_HEADS, HEAD_DIM))
        ctx_out[3].append(gqa[:N_CTX_TOK, GQ_W + GKV_W:].reshape(BATCH, SEQ, GQA_KV_HEADS, HEAD_DIM))
        ctx_out[4].append(c_new)
        ctx_out[5].append(n_new)
        ctx_out[6].append(m_new)

    y_prompt = x[:N_CTX_TOK].reshape(BATCH, SEQ, D_MODEL)
    y_sample = x[N_CTX_TOK:].reshape(DEC_BATCH, DEC_SEQ, D_MODEL)
    return (y_prompt, y_sample) + tuple(jnp.stack(o, axis=1) for o in ctx_out)
```
